```python
import math
import jax, jax.numpy as jnp
from jax import lax
import numpy as np

D_MODEL = 1024
BATCH = 8
SEQ = 2048
DEPTH = 1

RMS_EPS = 1e-5
ATTN_HEADS = 16
ATTN_KV_HEADS = 2
HEAD_DIM = 64
WINDOW = 128
Q_DIM = ATTN_HEADS * HEAD_DIM
KV_DIM = ATTN_KV_HEADS * HEAD_DIM
SSM_EXPAND = 2
D_INNER = SSM_EXPAND * D_MODEL
SSM_HEAD_DIM = 64
SSM_HEADS = D_INNER // SSM_HEAD_DIM
SSM_GROUPS = 4
D_STATE = 128
CONV_WIDTH = 4
CHUNK = 128
XBC_DIM = D_INNER + 2 * SSM_GROUPS * D_STATE
IN_COLS = Q_DIM + 2 * KV_DIM + D_INNER + XBC_DIM + SSM_HEADS
IN_SPLITS = (Q_DIM, Q_DIM + KV_DIM, Q_DIM + 2 * KV_DIM, Q_DIM + 2 * KV_DIM + D_INNER, Q_DIM + 2 * KV_DIM + D_INNER + XBC_DIM)
N_BRANCHES = 2
MEM_TOKENS = 256
CROSS_HEADS = 4
CROSS_HEAD_DIM = D_MODEL // CROSS_HEADS
N_EXPERTS = 32
TOP_K = 4
D_EXPERT = D_MODEL
SWIGLU_LIMIT = 7.0
SWIGLU_ALPHA = 1.702

kernel_name = "hybrid_swa_ssd_moe_block"


def rmsnorm(x, w):
    xf = x.astype(jnp.float32)
    y = xf * lax.rsqrt(jnp.mean(xf * xf, axis=-1, keepdims=True) + RMS_EPS)
    return (y * w.astype(jnp.float32)).astype(x.dtype)


def alibi_slopes(n_heads):
    return jnp.exp2(-8.0 * jnp.arange(1, n_heads + 1, dtype=jnp.float32) / n_heads)


def sliding_window_attention(q, k, v, sinks):
    bsz, seq, _ = q.shape
    nb = seq // WINDOW
    grp = ATTN_HEADS // ATTN_KV_HEADS
    qb = q.reshape(bsz, nb, WINDOW, ATTN_KV_HEADS, grp, HEAD_DIM)
    kb = k.reshape(bsz, nb, WINDOW, ATTN_KV_HEADS, HEAD_DIM)
    vb = v.reshape(bsz, nb, WINDOW, ATTN_KV_HEADS, HEAD_DIM)

    def with_prev(t):
        prev = jnp.concatenate([jnp.zeros_like(t[:, :1]), t[:, :-1]], axis=1)
        return jnp.concatenate([prev, t], axis=2)

    kk, vv = with_prev(kb), with_prev(vb)
    scores = jnp.einsum('bnqkgd,bnskd->bnkgqs', qb, kk).astype(jnp.float32) * (HEAD_DIM ** -0.5)
    qi = jnp.arange(WINDOW)[:, None]
    kj = jnp.arange(2 * WINDOW)[None, :]
    dist = qi + WINDOW - kj
    in_window = (dist >= 0) & (dist < WINDOW)
    has_prev = (jnp.arange(nb)[:, None, None] > 0) | (kj >= WINDOW)[None]
    mask = in_window[None] & has_prev
    slopes = alibi_slopes(ATTN_HEADS).reshape(ATTN_KV_HEADS, grp)
    bias = -slopes[:, :, None, None] * dist.astype(jnp.float32)[None, None]
    scores = jnp.where(mask[None, :, None, None], scores + bias, -jnp.inf)
    sink = sinks.astype(jnp.float32).reshape(ATTN_KV_HEADS, grp)[:, :, None, None]
    m = jnp.maximum(jnp.max(scores, axis=-1, keepdims=True), sink)
    p = jnp.exp(scores - m)
    probs = p / (jnp.sum(p, axis=-1, keepdims=True) + jnp.exp(sink - m))
    out = jnp.einsum('bnkgqs,bnskd->bnqkgd', probs.astype(v.dtype), vv)
    return out.reshape(bsz, seq, Q_DIM)


def causal_dwconv(u, w):
    ch = u.shape[-1]
    return lax.conv_general_dilated(u, w[:, None, :].astype(u.dtype), window_strides=(1,),
                                    padding=[(CONV_WIDTH - 1, 0)],
                                    dimension_numbers=('NWC', 'WIO', 'NWC'),
                                    feature_group_count=ch)


def ssd_chunked(xs, dt, a_heads, bm, cm):
    b, seq = xs.shape[0], xs.shape[1]
    nc = seq // CHUNK
    rep = SSM_HEADS // SSM_GROUPS
    x = xs.astype(jnp.float32).reshape(b, nc, CHUNK, SSM_GROUPS, rep, SSM_HEAD_DIM)
    dtc = dt.reshape(b, nc, CHUNK, SSM_GROUPS, rep)
    bc = bm.astype(jnp.float32).reshape(b, nc, CHUNK, SSM_GROUPS, D_STATE)
    cc = cm.astype(jnp.float32).reshape(b, nc, CHUNK, SSM_GROUPS, D_STATE)
    a = jnp.transpose(dtc * a_heads.reshape(SSM_GROUPS, rep), (0, 3, 4, 1, 2))
    a_cum = jnp.cumsum(a, axis=-1)
    xdt = x * dtc[..., None]
    causal = jnp.tril(jnp.ones((CHUNK, CHUNK), dtype=bool))
    seg = a_cum[..., :, None] - a_cum[..., None, :]
    lmat = jnp.exp(jnp.where(causal, seg, -jnp.inf))
    cb = jnp.einsum('bclgn,bcsgn->bgcls', cc, bc)
    y_diag = jnp.einsum('bgcls,bgrcls,bcsgrp->bclgrp', cb, lmat, xdt)
    decay_states = jnp.exp(a_cum[..., -1:] - a_cum)
    states = jnp.einsum('bclgn,bgrcl,bclgrp->cbgrpn', bc, decay_states, xdt)
    chunk_decay = jnp.moveaxis(jnp.exp(a_cum[..., -1]), -1, 0)

    def step(carry, inp):
        st, dec = inp
        return carry * dec[..., None, None] + st, carry

    _, prev = lax.scan(step, jnp.zeros(states.shape[1:], jnp.float32), (states, chunk_decay))
    y_off = jnp.einsum('bclgn,cbgrpn,bgrcl->bclgrp', cc, prev, jnp.exp(a_cum))
    y = (y_diag + y_off).reshape(b, seq, SSM_HEADS, SSM_HEAD_DIM)
    return y.astype(xs.dtype)


def mamba2_mixer(z, xbc, dt_raw, conv_w, conv_b, dt_bias, a_log, d_skip, norm_w):
    bsz, seq, _ = z.shape
    xbc = jax.nn.silu(causal_dwconv(xbc, conv_w) + conv_b)
    xs, bm, cm = jnp.split(xbc, [D_INNER, D_INNER + SSM_GROUPS * D_STATE], axis=-1)
    xs = xs.reshape(bsz, seq, SSM_HEADS, SSM_HEAD_DIM)
    bm = bm.reshape(bsz, seq, SSM_GROUPS, D_STATE)
    cm = cm.reshape(bsz, seq, SSM_GROUPS, D_STATE)
    dt = jax.nn.softplus(dt_raw.astype(jnp.float32) + dt_bias.astype(jnp.float32))
    a_heads = -jnp.exp(a_log.astype(jnp.float32))
    y = ssd_chunked(xs, dt, a_heads, bm, cm) + xs * d_skip[:, None]
    y = y.reshape(bsz, seq, D_INNER) * jax.nn.silu(z)
    yf = y.astype(jnp.float32).reshape(bsz, seq, SSM_GROUPS, D_INNER // SSM_GROUPS)
    yf = yf * lax.rsqrt(jnp.mean(yf * yf, axis=-1, keepdims=True) + RMS_EPS)
    return (yf.reshape(bsz, seq, D_INNER) * norm_w.astype(jnp.float32)).astype(z.dtype)


def cross_attention(hn, mn, w_cq, w_ckv, w_co):
    bsz, seq, _ = hn.shape
    q = (hn @ w_cq).reshape(bsz, seq, CROSS_HEADS, CROSS_HEAD_DIM)
    k, v = jnp.split(mn @ w_ckv, 2, axis=-1)
    k = k.reshape(bsz, -1, CROSS_HEADS, CROSS_HEAD_DIM)
    v = v.reshape(bsz, -1, CROSS_HEADS, CROSS_HEAD_DIM)
    s = jnp.einsum('bthd,bmhd->bhtm', q, k).astype(jnp.float32) * (CROSS_HEAD_DIM ** -0.5)
    p = jax.nn.softmax(s, axis=-1)
    o = jnp.einsum('bhtm,bmhd->bthd', p.astype(v.dtype), v).reshape(bsz, seq, D_MODEL)
    return o @ w_co


def moe_ffn(u, w_router, b_router, w_gu, b_gu, w_dn, b_dn):
    bsz, seq, _ = u.shape
    t = u.reshape(-1, D_MODEL)
    logits = (t @ w_router + b_router).astype(jnp.float32)
    top_v, top_i = lax.top_k(logits, TOP_K)
    top_w = jax.nn.softmax(top_v, axis=-1)
    gate = jnp.sum(jax.nn.one_hot(top_i, N_EXPERTS, dtype=jnp.float32) * top_w[..., None], axis=1)
    gate = gate.astype(t.dtype)
    y = jnp.zeros_like(t)
    for e in range(N_EXPERTS):
        gu = t @ w_gu[e] + b_gu[e]
        g, up = jnp.split(gu, 2, axis=-1)
        g = jnp.minimum(g, SWIGLU_LIMIT)
        up = jnp.clip(up, -SWIGLU_LIMIT, SWIGLU_LIMIT)
        act = (up + 1.0) * (g * jax.nn.sigmoid(SWIGLU_ALPHA * g))
        y = y + gate[:, e:e + 1] * (act @ w_dn[e] + b_dn[e])
    return y.reshape(bsz, seq, D_MODEL)


def setup_inputs(seed: int = 0) -> dict:
    key = jax.random.key(seed)
    ks = iter(jax.random.split(key, 40))
    f32 = jnp.float32
    L = DEPTH

    def dense(shape, fan_in):
        return jax.random.normal(next(ks), shape, f32) * (fan_in ** -0.5)

    def gain(shape):
        return 1.0 + 0.02 * jax.random.normal(next(ks), shape, f32)

    def small(shape, scale=0.01):
        return scale * jax.random.normal(next(ks), shape, f32)

    x = jax.random.normal(next(ks), (BATCH, SEQ, D_MODEL), f32)
    mem = jax.random.normal(next(ks), (BATCH, MEM_TOKENS, D_MODEL), f32)
    dt0 = jnp.exp(jax.random.uniform(next(ks), (L, SSM_HEADS), f32, math.log(1e-3), math.log(1e-1)))
    dt_bias = dt0 + jnp.log(-jnp.expm1(-dt0))
    a_log = jnp.log(jax.random.uniform(next(ks), (L, SSM_HEADS), f32, 1.0, 16.0))
    return {
        "x": x,
        "mem": mem,
        "norm_mix_w": gain((L, D_MODEL)),
        "w_in": dense((L, D_MODEL, IN_COLS), D_MODEL),
        "w_gate": dense((L, D_MODEL, N_BRANCHES * D_MODEL), D_MODEL),
        "b_gate": small((L, N_BRANCHES * D_MODEL), 0.1),
        "attn_sinks": small((L, ATTN_HEADS), 0.5),
        "conv_w": dense((L, CONV_WIDTH, XBC_DIM), CONV_WIDTH),
        "conv_b": small((L, XBC_DIM), 0.02),
        "dt_bias": dt_bias,
        "a_log": a_log,
        "d_skip": gain((L, SSM_HEADS)),
        "ssm_norm_w": gain((L, D_INNER)),
        "w_attn_o": dense((L, Q_DIM, D_MODEL), Q_DIM),
        "w_ssm_o": dense((L, D_INNER, D_MODEL), D_INNER),
        "w_out": dense((L, D_MODEL, D_MODEL), D_MODEL),
        "norm_cross_w": gain((L, D_MODEL)),
        "norm_mem_w": gain((L, D_MODEL)),
        "w_cq": dense((L, D_MODEL, D_MODEL), D_MODEL),
        "w_ckv": dense((L, D_MODEL, 2 * D_MODEL), D_MODEL),
        "w_co": dense((L, D_MODEL, D_MODEL), D_MODEL),
        "norm_ffn_w": gain((L, D_MODEL)),
        "w_router": dense((L, D_MODEL, N_EXPERTS), D_MODEL),
        "b_router": small((L, N_EXPERTS), 0.01),
        "w_gu": dense((L, N_EXPERTS, D_MODEL, 2 * D_EXPERT), D_MODEL),
        "b_gu": small((L, N_EXPERTS, 2 * D_EXPERT), 0.01),
        "w_dn": dense((L, N_EXPERTS, D_EXPERT, D_MODEL), D_EXPERT),
        "b_dn": small((L, N_EXPERTS, D_MODEL), 0.01),
        "final_norm_w": gain((D_MODEL,)),
    }


def reference(x, mem, norm_mix_w, w_in, w_gate, b_gate, attn_sinks, conv_w, conv_b, dt_bias, a_log, d_skip,
              ssm_norm_w, w_attn_o, w_ssm_o, w_out, norm_cross_w, norm_mem_w, w_cq, w_ckv, w_co,
              norm_ffn_w, w_router, b_router, w_gu, b_gu, w_dn, b_dn, final_norm_w):
    bsz, seq, _ = x.shape
    h = x
    for l in range(DEPTH):
        u = rmsnorm(h, norm_mix_w[l])
        q, k, v, z, xbc, dt_raw = jnp.split(u @ w_in[l], IN_SPLITS, axis=-1)
        attn = sliding_window_attention(q, k, v, attn_sinks[l])
        ssm = mamba2_mixer(z, xbc, dt_raw, conv_w[l], conv_b[l], dt_bias[l], a_log[l], d_skip[l], ssm_norm_w[l])
        gates = jax.nn.sigmoid(u @ w_gate[l] + b_gate[l]).reshape(bsz, seq, N_BRANCHES, D_MODEL)
        merged = gates[..., 0, :] * (attn @ w_attn_o[l]) + gates[..., 1, :] * (ssm @ w_ssm_o[l])
        h = h + merged @ w_out[l]
        h = h + cross_attention(rmsnorm(h, norm_cross_w[l]), rmsnorm(mem, norm_mem_w[l]), w_cq[l], w_ckv[l], w_co[l])
        h = h + moe_ffn(rmsnorm(h, norm_ffn_w[l]), w_router[l], b_router[l], w_gu[l], b_gu[l], w_dn[l], b_dn[l])
    return rmsnorm(h, final_norm_w)
```

```python
import functools

import jax
import jax.numpy as jnp
from jax import lax
from jax.experimental import pallas as pl
from jax.experimental.pallas import tpu as pltpu

BF = jnp.bfloat16
F32 = jnp.float32
I32 = jnp.int32

D_MODEL = 1024
RMS_EPS = 1e-5
ATTN_HEADS = 16
ATTN_KV_HEADS = 2
HEAD_DIM = 64
WINDOW = 128
Q_DIM = ATTN_HEADS * HEAD_DIM
KV_DIM = ATTN_KV_HEADS * HEAD_DIM
D_INNER = 2 * D_MODEL
SSM_HEADS = 32
SSM_GROUPS = 4
D_STATE = 128
CONV_WIDTH = 4
CHUNK = 128
XBC_DIM = D_INNER + 2 * SSM_GROUPS * D_STATE
MEM_TOKENS = 256
CROSS_HEADS = 4
CROSS_HEAD_DIM = D_MODEL // CROSS_HEADS
N_EXPERTS = 32
TOP_K = 4
D_EXPERT = D_MODEL
SWIGLU_LIMIT = 7.0
SWIGLU_ALPHA = 1.702

LANES = 128
SUBLANES = 8
ROW_TILES = D_MODEL // LANES
NEG_BIG = -1e30
VMEM_LIMIT = 56 * 1024 * 1024

PROJ_TM = 256
OUT_TM = 512
CROSS_TM = 512
ROUTER_TM = 512
DISPATCH_TT = 512
EXPERT_TM = 256
COMBINE_TT = 256


def _rms(x, w):
    return x * lax.rsqrt(jnp.mean(x * x, axis=-1, keepdims=True) + RMS_EPS) * w


def _dot(a, b):
    return jnp.dot(a, b, preferred_element_type=F32)


def _dot_nt(a, b):
    return lax.dot_general(a, b, (((1,), (1,)), ((), ())), preferred_element_type=F32)


def _dot_f32(a, b):
    return jnp.dot(a, b, preferred_element_type=F32, precision=lax.Precision.HIGHEST)


def _params(sem):
    return pltpu.CompilerParams(dimension_semantics=sem, vmem_limit_bytes=VMEM_LIMIT)


def _full(shape):
    nd = len(shape)
    return pl.BlockSpec(shape, lambda *_: (0,) * nd)


def _proj_kernel(x_ref, nw_ref, wq_ref, wkv_ref, wz_ref, wxbc_ref, wdt_ref, wg_ref, bg_ref,
                 q_ref, kv_ref, z_ref, xbc_ref, dt_ref, g_ref):
    u = _rms(x_ref[...], nw_ref[...]).astype(BF)
    q_ref[...] = _dot(u, wq_ref[...]).astype(BF)
    kv_ref[...] = _dot(u, wkv_ref[...]).astype(BF)
    z_ref[...] = _dot(u, wz_ref[...]).astype(BF)
    xbc_ref[...] = _dot(u, wxbc_ref[...]).astype(BF)
    dt_ref[...] = _dot(u, wdt_ref[...])
    g_ref[...] = jax.nn.sigmoid(_dot(u, wg_ref[...]) + bg_ref[...]).astype(BF)


def _proj(x2, norm_w, w_in, w_gate, b_gate):
    n = x2.shape[0]
    tm = PROJ_TM
    s0, s1, s2, s3, s4 = Q_DIM, Q_DIM + KV_DIM, Q_DIM + 2 * KV_DIM, Q_DIM + 2 * KV_DIM + D_INNER, \
        Q_DIM + 2 * KV_DIM + D_INNER + XBC_DIM
    wq = w_in[:, :s0].astype(BF)
    wk, wv = w_in[:, s0:s1], w_in[:, s1:s2]
    dup = lambda w: jnp.concatenate([w[:, :HEAD_DIM], w[:, :HEAD_DIM], w[:, HEAD_DIM:], w[:, HEAD_DIM:]], axis=1)
    wkv = jnp.concatenate([dup(wk), dup(wv)], axis=1).astype(BF)
    wz = w_in[:, s2:s3].astype(BF)
    wxbc = w_in[:, s3:s4].astype(BF)
    wdt = jnp.pad(w_in[:, s4:], ((0, 0), (0, LANES - SSM_HEADS))).astype(BF)
    wg = w_gate.astype(BF)
    outs = (
        jax.ShapeDtypeStruct((n, Q_DIM), BF),
        jax.ShapeDtypeStruct((n, 4 * LANES), BF),
        jax.ShapeDtypeStruct((n, D_INNER), BF),
        jax.ShapeDtypeStruct((n, XBC_DIM), BF),
        jax.ShapeDtypeStruct((n, LANES), F32),
        jax.ShapeDtypeStruct((n, 2 * D_MODEL), BF),
    )
    row = lambda w: pl.BlockSpec((tm, w), lambda i: (i, 0))
    return pl.pallas_call(
        _proj_kernel,
        out_shape=outs,
        grid=(n // tm,),
        in_specs=[row(D_MODEL), _full((1, D_MODEL)), _full(wq.shape), _full(wkv.shape), _full(wz.shape),
                  _full(wxbc.shape), _full(wdt.shape), _full(wg.shape), _full((1, 2 * D_MODEL))],
        out_specs=[row(Q_DIM), row(4 * LANES), row(D_INNER), row(XBC_DIM), row(LANES), row(2 * D_MODEL)],
        compiler_params=_params(("parallel",)),
        name="proj",
    )(x2, norm_w.reshape(1, -1), wq, wkv, wz, wxbc, wdt, wg, b_gate.reshape(1, -1))


def _swa_kernel(sink_ref, q_ref, kvc_ref, kvp_ref, o_ref):
    n = pl.program_id(1)
    w = WINDOW
    qi = lax.broadcasted_iota(I32, (w, w), 0)
    kj = lax.broadcasted_iota(I32, (w, w), 1)
    dcur = qi - kj
    dprev = dcur + w
    mcur = dcur >= 0
    mprev = (dprev < w) & (n > 0)
    dcur_f = dcur.astype(F32)
    dprev_f = dprev.astype(F32)
    upper = lax.broadcasted_iota(I32, (w, LANES), 1) >= HEAD_DIM
    zero = jnp.zeros((w, LANES), BF)
    halves = lambda t: (jnp.where(upper, zero, t), jnp.where(upper, t, zero))
    scale = HEAD_DIM ** -0.5
    pairs_per_kv = ATTN_HEADS // ATTN_KV_HEADS // 2
    for kvh in range(ATTN_KV_HEADS):
        kc = halves(kvc_ref[:, LANES * kvh:LANES * (kvh + 1)])
        kp = halves(kvp_ref[:, LANES * kvh:LANES * (kvh + 1)])
        vc = halves(kvc_ref[:, LANES * (2 + kvh):LANES * (3 + kvh)])
        vp = halves(kvp_ref[:, LANES * (2 + kvh):LANES * (3 + kvh)])
        for pr in range(pairs_per_kv):
            pidx = kvh * pairs_per_kv + pr
            qp = q_ref[:, LANES * pidx:LANES * (pidx + 1)]
            acc = None
            for r in range(2):
                h = 2 * pidx + r
                slope = 2.0 ** (-8.0 * (h + 1) / ATTN_HEADS)
                sink = sink_ref[h]
                sc = jnp.where(mcur, _dot_nt(qp, kc[r]) * scale - slope * dcur_f, NEG_BIG)
                sp = jnp.where(mprev, _dot_nt(qp, kp[r]) * scale - slope * dprev_f, NEG_BIG)
                m = jnp.maximum(jnp.maximum(jnp.max(sc, axis=-1, keepdims=True),
                                            jnp.max(sp, axis=-1, keepdims=True)), sink)
                pc = jnp.exp(sc - m)
                pp = jnp.exp(sp - m)
                den = (jnp.sum(pc, axis=-1, keepdims=True) + jnp.sum(pp, axis=-1, keepdims=True)
                       + jnp.exp(sink - m))
                o = (_dot(pc.astype(BF), vc[r]) + _dot(pp.astype(BF), vp[r])) / den
                acc = o if acc is None else acc + o
            o_ref[:, LANES * pidx:LANES * (pidx + 1)] = acc.astype(BF)


def _swa(q, kv, sinks, bsz, seq):
    nb = seq // WINDOW
    n = bsz * seq
    return pl.pallas_call(
        _swa_kernel,
        out_shape=jax.ShapeDtypeStruct((n, Q_DIM), BF),
        grid=(bsz, nb),
        in_specs=[
            pl.BlockSpec(memory_space=pltpu.SMEM),
            pl.BlockSpec((WINDOW, Q_DIM), lambda b, j: (b * nb + j, 0)),
            pl.BlockSpec((WINDOW, 4 * LANES), lambda b, j: (b * nb + j, 0)),
            pl.BlockSpec((WINDOW, 4 * LANES), lambda b, j: (jnp.maximum(b * nb + j - 1, 0), 0)),
        ],
        out_specs=pl.BlockSpec((WINDOW, Q_DIM), lambda b, j: (b * nb + j, 0)),
        compiler_params=_params(("parallel", "parallel")),
        name="swa",
    )(sinks.astype(F32), q, kv, kv)


def _softplus(x):
    return jnp.maximum(x, 0.0) + jnp.log1p(jnp.exp(-jnp.abs(x)))


def _ssd_kernel(xbc_ref, z_ref, dtc_ref, dtr_ref, cw_ref, cb_ref, dtbc_ref, dtbr_ref, alc_ref, alr_ref,
                dskip_ref, nw_ref, o_ref, state_ref, tail_ref):
    c = pl.program_id(1)
    L = CHUNK

    @pl.when(c == 0)
    def _():
        state_ref[...] = jnp.zeros_like(state_ref)
        tail_ref[...] = jnp.zeros_like(tail_ref)

    x_raw = xbc_ref[...].astype(F32)
    prev8 = tail_ref[...]
    row8 = lax.broadcasted_iota(I32, (SUBLANES, XBC_DIM), 0)
    acc = x_raw * cw_ref[CONV_WIDTH - 1:CONV_WIDTH, :] + cb_ref[...]
    for s in range(1, CONV_WIDTH):
        sh = pltpu.roll(x_raw, s, axis=0)
        fix = pltpu.roll(prev8, s, axis=0)
        top = jnp.where(row8 < s, fix, sh[:SUBLANES])
        sh = jnp.concatenate([top, sh[SUBLANES:]], axis=0)
        acc = acc + sh * cw_ref[CONV_WIDTH - 1 - s:CONV_WIDTH - s, :]
    tail_ref[...] = x_raw[L - SUBLANES:, :]
    xc = acc * jax.nn.sigmoid(acc)

    dt_col = _softplus(dtc_ref[...] + dtbc_ref[...])
    dt_row = _softplus(dtr_ref[...] + dtbr_ref[...])
    a_col = dt_col * (-jnp.exp(alc_ref[...]))
    a_row = dt_row * (-jnp.exp(alr_ref[...]))
    ii = lax.broadcasted_iota(I32, (L, L), 0)
    jj = lax.broadcasted_iota(I32, (L, L), 1)
    causal = ii >= jj
    tri_l = jnp.where(causal, 1.0, 0.0).astype(F32)
    tri_u = jnp.where(ii <= jj, 1.0, 0.0).astype(F32)
    acum_col = _dot_f32(tri_l, a_col)
    acum_row = _dot_f32(a_row, tri_u)
    last_col = acum_col[L - 1:L, :]
    last_row = acum_row[:, L - 1:L]
    wrow = jnp.exp(last_row - acum_row) * dt_row

    upper = lax.broadcasted_iota(I32, (L, LANES), 1) >= 64
    upper1 = upper[:1]
    heads_per_group = SSM_HEADS // SSM_GROUPS
    group_w = D_INNER // SSM_GROUPS
    for g in range(SSM_GROUPS):
        b_g = xc[:, D_INNER + D_STATE * g:D_INNER + D_STATE * (g + 1)]
        c_g = xc[:, D_INNER + D_STATE * (SSM_GROUPS + g):D_INNER + D_STATE * (SSM_GROUPS + g + 1)]
        cb = _dot_nt(c_g.astype(BF), b_g.astype(BF))
        b_gt = b_g.T
        ys = []
        for j in range(heads_per_group // 2):
            pidx = g * (heads_per_group // 2) + j
            x_pair = xc[:, LANES * pidx:LANES * (pidx + 1)]
            st_pair = state_ref[pidx]
            acc_y = None
            acc_s = None
            for r in range(2):
                h = 2 * pidx + r
                acol = jnp.broadcast_to(acum_col[:, h:h + 1], (L, L))
                arow = acum_row[h:h + 1, :]
                lmat = jnp.exp(jnp.where(causal, acol - arow, NEG_BIG))
                m_h = cb * lmat * dt_row[h:h + 1, :]
                c_h = c_g * jnp.exp(acol)
                lhs = jnp.concatenate([m_h, c_h], axis=1).astype(BF)
                keep = upper if r == 1 else jnp.logical_not(upper)
                xm = jnp.where(keep, x_pair, 0.0).astype(BF)
                sm = jnp.where(keep, st_pair, 0.0).astype(BF)
                rhs = jnp.concatenate([xm, sm], axis=0)
                y = _dot(lhs, rhs)
                acc_y = y if acc_y is None else acc_y + y
                bw = (b_gt * wrow[h:h + 1, :]).astype(BF)
                sn = _dot(bw, xm)
                acc_s = sn if acc_s is None else acc_s + sn
            d0 = jnp.broadcast_to(jnp.exp(last_col[:, 2 * pidx:2 * pidx + 1]), (1, LANES))
            d1 = jnp.broadcast_to(jnp.exp(last_col[:, 2 * pidx + 1:2 * pidx + 2]), (1, LANES))
            state_ref[pidx] = st_pair * jnp.where(upper1, d1, d0) + acc_s
            ys.append(acc_y + x_pair * dskip_ref[:, LANES * pidx:LANES * (pidx + 1)])
        y_g = jnp.concatenate(ys, axis=1)
        z_g = z_ref[:, group_w * g:group_w * (g + 1)].astype(F32)
        y_g = y_g * (z_g * jax.nn.sigmoid(z_g))
        y_g = _rms(y_g, nw_ref[:, group_w * g:group_w * (g + 1)])
        o_ref[:, group_w * g:group_w * (g + 1)] = y_g.astype(BF)


def _ssd(xbc, z, dt_raw, conv_w, conv_b, dt_bias, a_log, d_skip, norm_w, bsz, seq):
    nc = seq // CHUNK
    n = bsz * seq
    pad = LANES - SSM_HEADS
    dt_row = dt_raw[:, :SSM_HEADS].T
    dtb_col = jnp.pad(dt_bias, (0, pad)).reshape(1, LANES)
    al_col = jnp.pad(a_log, (0, pad)).reshape(1, LANES)
    dtb_row = jnp.broadcast_to(dt_bias[:, None], (SSM_HEADS, CHUNK))
    al_row = jnp.broadcast_to(a_log[:, None], (SSM_HEADS, CHUNK))
    dskip = jnp.repeat(d_skip, D_INNER // SSM_HEADS).reshape(1, D_INNER)
    blk = lambda w: pl.BlockSpec((CHUNK, w), lambda b, j: (b * nc + j, 0))
    return pl.pallas_call(
        _ssd_kernel,
        out_shape=jax.ShapeDtypeStruct((n, D_INNER), BF),
        grid=(bsz, nc),
        in_specs=[
            blk(XBC_DIM), blk(D_INNER), blk(LANES),
            pl.BlockSpec((SSM_HEADS, CHUNK), lambda b, j: (0, b * nc + j)),
            _full((CONV_WIDTH, XBC_DIM)), _full((1, XBC_DIM)),
            _full((1, LANES)), _full((SSM_HEADS, CHUNK)), _full((1, LANES)), _full((SSM_HEADS, CHUNK)),
            _full((1, D_INNER)), _full((1, D_INNER)),
        ],
        out_specs=blk(D_INNER),
        scratch_shapes=[pltpu.VMEM((SSM_HEADS // 2, D_STATE, LANES), F32),
                        pltpu.VMEM((SUBLANES, XBC_DIM), F32)],
        compiler_params=_params(("parallel", "arbitrary")),
        name="ssd",
    )(xbc, z, dt_raw, dt_row, conv_w, conv_b.reshape(1, -1), dtb_col, dtb_row, al_col, al_row,
      dskip, norm_w.reshape(1, -1))


def _outproj_kernel(x_ref, attn_ref, ssm_ref, g_ref, wa_ref, ws_ref, wo_ref, h_ref):
    a = _dot(attn_ref[...], wa_ref[...])
    s = _dot(ssm_ref[...], ws_ref[...])
    merged = g_ref[:, :D_MODEL].astype(F32) * a + g_ref[:, D_MODEL:].astype(F32) * s
    h_ref[...] = x_ref[...] + _dot(merged.astype(BF), wo_ref[...])


def _outproj(x2, attn, ssm, gates, w_attn_o, w_ssm_o, w_out):
    n = x2.shape[0]
    tm = OUT_TM
    row = lambda w: pl.BlockSpec((tm, w), lambda i: (i, 0))
    return pl.pallas_call(
        _outproj_kernel,
        out_shape=jax.ShapeDtypeStruct((n, D_MODEL), F32),
        grid=(n // tm,),
        in_specs=[row(D_MODEL), row(Q_DIM), row(D_INNER), row(2 * D_MODEL),
                  _full((Q_DIM, D_MODEL)), _full((D_INNER, D_MODEL)), _full((D_MODEL, D_MODEL))],
        out_specs=row(D_MODEL),
        compiler_params=_params(("parallel",)),
        name="outproj",
    )(x2, attn, ssm, gates, w_attn_o.astype(BF), w_ssm_o.astype(BF), w_out.astype(BF))


def _memkv_kernel(mem_ref, nw_ref, w_ref, kv_ref):
    mn = _rms(mem_ref[...], nw_ref[...]).astype(BF)
    kv_ref[...] = _dot(mn, w_ref[...]).astype(BF)


def _memkv(mem2, norm_w, w_ckv, bsz):
    return pl.pallas_call(
        _memkv_kernel,
        out_shape=jax.ShapeDtypeStruct((bsz * MEM_TOKENS, 2 * D_MODEL), BF),
        grid=(bsz,),
        in_specs=[pl.BlockSpec((MEM_TOKENS, D_MODEL), lambda b: (b, 0)), _full((1, D_MODEL)),
                  _full((D_MODEL, 2 * D_MODEL))],
        out_specs=pl.BlockSpec((MEM_TOKENS, 2 * D_MODEL), lambda b: (b, 0)),
        compiler_params=_params(("parallel",)),
        name="memkv",
    )(mem2, norm_w.reshape(1, -1), w_ckv.astype(BF))


def _cross_kernel(h_ref, kv_ref, nw_ref, wq_ref, wo_ref, o_ref):
    h = h_ref[...]
    hn = _rms(h, nw_ref[...]).astype(BF)
    q = (_dot(hn, wq_ref[...]) * (CROSS_HEAD_DIM ** -0.5)).astype(BF)
    outs = []
    for hd in range(CROSS_HEADS):
        lo, hi = CROSS_HEAD_DIM * hd, CROSS_HEAD_DIM * (hd + 1)
        s = _dot_nt(q[:, lo:hi], kv_ref[:, lo:hi])
        p = jnp.exp(s - jnp.max(s, axis=-1, keepdims=True))
        den = jnp.sum(p, axis=-1, keepdims=True)
        outs.append((_dot(p.astype(BF), kv_ref[:, D_MODEL + lo:D_MODEL + hi]) / den).astype(BF))
    o = jnp.concatenate(outs, axis=1)
    o_ref[...] = h + _dot(o, wo_ref[...])


def _cross(h, kv, norm_w, w_cq, w_co, seq):
    n = h.shape[0]
    tm = CROSS_TM
    per_b = seq // tm
    row = pl.BlockSpec((tm, D_MODEL), lambda i: (i, 0))
    return pl.pallas_call(
        _cross_kernel,
        out_shape=jax.ShapeDtypeStruct((n, D_MODEL), F32),
        grid=(n // tm,),
        in_specs=[row, pl.BlockSpec((MEM_TOKENS, 2 * D_MODEL), lambda i: (i // per_b, 0)),
                  _full((1, D_MODEL)), _full((D_MODEL, D_MODEL)), _full((D_MODEL, D_MODEL))],
        out_specs=row,
        compiler_params=_params(("parallel",)),
        name="cross",
    )(h, kv, norm_w.reshape(1, -1), w_cq.astype(BF), w_co.astype(BF))


def _router_kernel(h_ref, nw_ref, wr_ref, br_ref, t_ref, meta_ref, cnt_ref, carry_ref):
    i = pl.program_id(0)
    tm = ROUTER_TM

    @pl.when(i == 0)
    def _():
        carry_ref[...] = jnp.zeros_like(carry_ref)

    t = _rms(h_ref[...], nw_ref[...])
    for c in range(ROW_TILES):
        t_ref[pl.ds(c, tm, stride=ROW_TILES), :] = t[:, LANES * c:LANES * (c + 1)]
    lane = lax.broadcasted_iota(I32, (tm, LANES), 1)
    logits = jnp.where(lane < N_EXPERTS, _dot_f32(t, wr_ref[...]) + br_ref[...], NEG_BIG)
    work = logits
    vals, ids, hots = [], [], []
    for _k in range(TOP_K):
        m = jnp.max(work, axis=-1, keepdims=True)
        idx = jnp.min(jnp.where(work == m, lane, LANES), axis=-1, keepdims=True)
        hot = lane == idx
        vals.append(m)
        ids.append(idx)
        hots.append(hot)
        work = jnp.where(hot, NEG_BIG * 2.0, work)
    es = [jnp.exp(v - vals[0]) for v in vals]
    den = es[0] + es[1] + es[2] + es[3]
    member = jnp.zeros((tm, LANES), F32)
    for hot in hots:
        member = member + jnp.where(hot, 1.0, 0.0)
    ii = lax.broadcasted_iota(I32, (tm, tm), 0)
    jj = lax.broadcasted_iota(I32, (tm, tm), 1)
    strict = jnp.where(ii > jj, 1.0, 0.0).astype(BF)
    carry = carry_ref[0:1, :]
    before = _dot(strict, member.astype(BF)) + carry
    meta = jnp.zeros((tm, LANES), F32)
    for k in range(TOP_K):
        rank = jnp.sum(jnp.where(hots[k], before, 0.0), axis=-1, keepdims=True)
        meta = jnp.where(lane == k, ids[k].astype(F32), meta)
        meta = jnp.where(lane == TOP_K + k, rank, meta)
        meta = jnp.where(lane == 2 * TOP_K + k, es[k] / den, meta)
    meta_ref[...] = meta
    new_carry = carry + jnp.sum(member, axis=0, keepdims=True)
    carry_ref[...] = jnp.broadcast_to(new_carry, carry_ref.shape)
    cnt_ref[...] = jnp.broadcast_to(new_carry, cnt_ref.shape)


def _router(h, norm_w, w_router, b_router):
    n = h.shape[0]
    tm = ROUTER_TM
    wr = jnp.pad(w_router, ((0, 0), (0, LANES - N_EXPERTS)))
    br = jnp.pad(b_router, (0, LANES - N_EXPERTS)).reshape(1, LANES)
    return pl.pallas_call(
        _router_kernel,
        out_shape=(jax.ShapeDtypeStruct((n * ROW_TILES, LANES), F32),
                   jax.ShapeDtypeStruct((n, LANES), F32),
                   jax.ShapeDtypeStruct((SUBLANES, LANES), F32)),
        grid=(n // tm,),
        in_specs=[pl.BlockSpec((tm, D_MODEL), lambda i: (i, 0)), _full((1, D_MODEL)),
                  _full((D_MODEL, LANES)), _full((1, LANES))],
        out_specs=[pl.BlockSpec((tm * ROW_TILES, LANES), lambda i: (i, 0)),
                   pl.BlockSpec((tm, LANES), lambda i: (i, 0)),
                   _full((SUBLANES, LANES))],
        scratch_shapes=[pltpu.VMEM((SUBLANES, LANES), F32)],
        compiler_params=_params(("arbitrary",)),
        name="router",
    )(h, norm_w.reshape(1, -1), wr, br)


def _row(ref, r):
    return ref.at[pl.ds(pl.multiple_of(r * ROW_TILES, ROW_TILES), ROW_TILES)]


def _dispatch_kernel(dest_ref, t_hbm, xs_hbm, sem):
    i = pl.program_id(0)
    tt = DISPATCH_TT

    def copy(j, k):
        return pltpu.make_async_copy(_row(t_hbm, i * tt + j), _row(xs_hbm, dest_ref[j * TOP_K + k]), sem)

    def issue(j, carry):
        for k in range(TOP_K):
            copy(j, k).start()
        return carry

    def drain(j, carry):
        for k in range(TOP_K):
            copy(j, k).wait()
        return carry

    lax.fori_loop(0, tt, issue, 0)
    lax.fori_loop(0, tt, drain, 0)


def _dispatch(t_rows, dest):
    n = t_rows.shape[0] // ROW_TILES
    tt = DISPATCH_TT
    return pl.pallas_call(
        _dispatch_kernel,
        out_shape=jax.ShapeDtypeStruct((n * TOP_K * ROW_TILES, LANES), F32),
        grid=(n // tt,),
        in_specs=[pl.BlockSpec((tt * TOP_K,), lambda i: (i,), memory_space=pltpu.SMEM),
                  pl.BlockSpec(memory_space=pl.ANY)],
        out_specs=pl.BlockSpec(memory_space=pl.ANY),
        scratch_shapes=[pltpu.SemaphoreType.DMA],
        compiler_params=_params(("arbitrary",)),
        name="dispatch",
    )(dest, t_rows)


def _expert_kernel(tile_ref, exp_ref, lo_ref, hi_ref, first_ref, xs_ref, wgu_ref, bgu_ref, wdn_ref, bdn_ref,
                   ys_ref):
    w = pl.program_id(0)
    tm = EXPERT_TM
    lo = lo_ref[w]
    hi = hi_ref[w]

    @pl.when(hi > lo)
    def _():
        x = jnp.concatenate([xs_ref[pl.ds(c, tm, stride=ROW_TILES), :] for c in range(ROW_TILES)],
                            axis=1).astype(BF)
        gu = _dot(x, wgu_ref[0]) + bgu_ref[0]
        g = jnp.minimum(gu[:, :D_EXPERT], SWIGLU_LIMIT)
        up = jnp.clip(gu[:, D_EXPERT:], -SWIGLU_LIMIT, SWIGLU_LIMIT)
        act = (up + 1.0) * (g * jax.nn.sigmoid(SWIGLU_ALPHA * g))
        y = _dot(act.astype(BF), wdn_ref[0]) + bdn_ref[0]
        rows = tile_ref[w] * tm + lax.broadcasted_iota(I32, (tm, LANES), 0)
        mine = (rows >= lo) & (rows < hi)
        is_first = first_ref[w] == 1

        @pl.when(is_first)
        def _():
            for c in range(ROW_TILES):
                ys_ref[pl.ds(c, tm, stride=ROW_TILES), :] = jnp.where(mine, y[:, LANES * c:LANES * (c + 1)], 0.0)

        @pl.when(jnp.logical_not(is_first))
        def _():
            for c in range(ROW_TILES):
                old = ys_ref[pl.ds(c, tm, stride=ROW_TILES), :]
                ys_ref[pl.ds(c, tm, stride=ROW_TILES), :] = jnp.where(mine, y[:, LANES * c:LANES * (c + 1)], old)


def _experts(xs, items, w_gu, b_gu, w_dn, b_dn):
    tile_of, exp_of, lo, hi, first = items
    n_items = tile_of.shape[0]
    tm = EXPERT_TM
    grid_spec = pltpu.PrefetchScalarGridSpec(
        num_scalar_prefetch=5,
        grid=(n_items,),
        in_specs=[
            pl.BlockSpec((tm * ROW_TILES, LANES), lambda w, t, e, *_: (t[w], 0)),
            pl.BlockSpec((1, D_MODEL, 2 * D_EXPERT), lambda w, t, e, *_: (e[w], 0, 0)),
            pl.BlockSpec((1, 1, 2 * D_EXPERT), lambda w, t, e, *_: (e[w], 0, 0)),
            pl.BlockSpec((1, D_EXPERT, D_MODEL), lambda w, t, e, *_: (e[w], 0, 0)),
            pl.BlockSpec((1, 1, D_MODEL), lambda w, t, e, *_: (e[w], 0, 0)),
        ],
        out_specs=pl.BlockSpec((tm * ROW_TILES, LANES), lambda w, t, e, *_: (t[w], 0)),
    )
    return pl.pallas_call(
        _expert_kernel,
        out_shape=jax.ShapeDtypeStruct(xs.shape, F32),
        grid_spec=grid_spec,
        compiler_params=_params(("arbitrary",)),
        name="experts",
    )(tile_of, exp_of, lo, hi, first, xs, w_gu.astype(BF), b_gu.reshape(N_EXPERTS, 1, -1),
      w_dn.astype(BF), b_dn.reshape(N_EXPERTS, 1, -1))


def _expert_items(counts, n_rows):
    tm = EXPERT_TM
    n_tiles = n_rows // tm
    n_items = n_tiles + N_EXPERTS - 1
    end = jnp.cumsum(counts)
    off = end - counts
    first_tile = off // tm
    last_tile = jnp.maximum(end - 1, 0) // tm
    per = jnp.where(counts > 0, last_tile - first_tile + 1, 0)
    cum = jnp.cumsum(per)
    start = cum - per
    total = cum[-1]
    w = jnp.arange(n_items, dtype=I32)
    wc = jnp.minimum(w, total - 1)
    e_of = jnp.minimum(jnp.searchsorted(cum, wc, side="right"), N_EXPERTS - 1).astype(I32)
    t_of = (first_tile[e_of] + (wc - start[e_of])).astype(I32)
    valid = w < total
    lo = jnp.where(valid, jnp.maximum(off[e_of], t_of * tm), 0).astype(I32)
    hi = jnp.where(valid, jnp.minimum(end[e_of], (t_of + 1) * tm), 0).astype(I32)
    prev_t = jnp.concatenate([jnp.full((1,), -1, I32), t_of[:-1]])
    first = (t_of != prev_t).astype(I32)
    return t_of, e_of, lo, hi, first


def _combine_kernel(dest_ref, meta_ref, h_ref, fw_ref, ys_hbm, o_ref, buf_ref, sem):
    tt = COMBINE_TT

    def copy(j, k):
        return pltpu.make_async_copy(_row(ys_hbm, dest_ref[j * TOP_K + k]), _row(buf_ref, k * tt + j), sem)

    def issue(j, carry):
        for k in range(TOP_K):
            copy(j, k).start()
        return carry

    def drain(j, carry):
        for k in range(TOP_K):
            copy(j, k).wait()
        return carry

    lax.fori_loop(0, tt, issue, 0)
    lax.fori_loop(0, tt, drain, 0)
    meta = meta_ref[...]
    wts = [jnp.broadcast_to(meta[:, 2 * TOP_K + k:2 * TOP_K + k + 1], (tt, LANES)) for k in range(TOP_K)]
    cols = []
    for c in range(ROW_TILES):
        acc = h_ref[:, LANES * c:LANES * (c + 1)]
        for k in range(TOP_K):
            acc = acc + wts[k] * buf_ref[pl.ds(k * tt * ROW_TILES + c, tt, stride=ROW_TILES), :]
        cols.append(acc)
    o_ref[...] = _rms(jnp.concatenate(cols, axis=1), fw_ref[...])


def _combine(dest, meta, h, final_w, ys):
    n = h.shape[0]
    tt = COMBINE_TT
    return pl.pallas_call(
        _combine_kernel,
        out_shape=jax.ShapeDtypeStruct((n, D_MODEL), F32),
        grid=(n // tt,),
        in_specs=[pl.BlockSpec((tt * TOP_K,), lambda i: (i,), memory_space=pltpu.SMEM),
                  pl.BlockSpec((tt, LANES), lambda i: (i, 0)),
                  pl.BlockSpec((tt, D_MODEL), lambda i: (i, 0)),
                  _full((1, D_MODEL)),
                  pl.BlockSpec(memory_space=pl.ANY)],
        out_specs=pl.BlockSpec((tt, D_MODEL), lambda i: (i, 0)),
        scratch_shapes=[pltpu.VMEM((TOP_K * tt * ROW_TILES, LANES), F32), pltpu.SemaphoreType.DMA],
        compiler_params=_params(("arbitrary",)),
        name="combine",
    )(dest, meta, h, final_w.reshape(1, -1), ys)


def _moe(h, norm_w, w_router, b_router, w_gu, b_gu, w_dn, b_dn, final_w):
    n = h.shape[0]
    t_rows, meta, cnt = _router(h, norm_w, w_router, b_router)
    eid = meta[:, :TOP_K].astype(I32)
    rank = meta[:, TOP_K:2 * TOP_K].astype(I32)
    counts = cnt[0, :N_EXPERTS].astype(I32)
    off = jnp.cumsum(counts) - counts
    dest = (off[eid] + rank).reshape(-1)
    xs = _dispatch(t_rows, dest)
    ys = _experts(xs, _expert_items(counts, n * TOP_K), w_gu, b_gu, w_dn, b_dn)
    return _combine(dest, meta, h, final_w, ys)


def kernel(x, mem, norm_mix_w, w_in, w_gate, b_gate, attn_sinks, conv_w, conv_b, dt_bias, a_log, d_skip,
           ssm_norm_w, w_attn_o, w_ssm_o, w_out, norm_cross_w, norm_mem_w, w_cq, w_ckv, w_co,
           norm_ffn_w, w_router, b_router, w_gu, b_gu, w_dn, b_dn, final_norm_w):
    bsz, seq, _ = x.shape
    assert norm_mix_w.shape[0] == 1, "single-layer block"
    x2 = x.reshape(bsz * seq, D_MODEL)
    q, kv, z, xbc, dt_raw, gates = _proj(x2, norm_mix_w[0], w_in[0], w_gate[0], b_gate[0])
    attn = _swa(q, kv, attn_sinks[0], bsz, seq)
    ssm = _ssd(xbc, z, dt_raw, conv_w[0], conv_b[0], dt_bias[0], a_log[0], d_skip[0], ssm_norm_w[0], bsz, seq)
    h = _outproj(x2, attn, ssm, gates, w_attn_o[0], w_ssm_o[0], w_out[0])
    mkv = _memkv(mem.reshape(bsz * MEM_TOKENS, D_MODEL), norm_mem_w[0], w_ckv[0], bsz)
    h = _cross(h, mkv, norm_cross_w[0], w_cq[0], w_co[0], seq)
    out = _moe(h, norm_ffn_w[0], w_router[0], b_router[0], w_gu[0], b_gu[0], w_dn[0], b_dn[0], final_norm_w)
    return out.reshape(bsz, seq, D_MODEL)
```

```python
import functools

import jax
import jax.numpy as jnp
from jax import lax
from jax.experimental import pallas as pl
from jax.experimental.pallas import tpu as pltpu

BF = jnp.bfloat16
F32 = jnp.float32
I32 = jnp.int32

D_MODEL = 1024
RMS_EPS = 1e-5
ATTN_HEADS = 16
ATTN_KV_HEADS = 2
HEAD_DIM = 64
WINDOW = 128
Q_DIM = ATTN_HEADS * HEAD_DIM
KV_DIM = ATTN_KV_HEADS * HEAD_DIM
D_INNER = 2 * D_MODEL
SSM_HEADS = 32
SSM_GROUPS = 4
D_STATE = 128
CONV_WIDTH = 4
CHUNK = 128
XBC_DIM = D_INNER + 2 * SSM_GROUPS * D_STATE
MEM_TOKENS = 256
CROSS_HEADS = 4
CROSS_HEAD_DIM = D_MODEL // CROSS_HEADS
N_EXPERTS = 32
TOP_K = 4
D_EXPERT = D_MODEL
SWIGLU_LIMIT = 7.0
SWIGLU_ALPHA = 1.702

LANES = 128
SUBLANES = 8
ROW_TILES = D_MODEL // LANES
NEG_BIG = -1e30
VMEM_LIMIT = 56 * 1024 * 1024

PROJ_TM = 256
OUT_TM = 512
CROSS_TM = 512
ROUTER_TM = 512
DISPATCH_TT = 512
EXPERT_TM = 256
COMBINE_TT = 256


def _rms(x, w):
    return x * lax.rsqrt(jnp.mean(x * x, axis=-1, keepdims=True) + RMS_EPS) * w


def _dot(a, b):
    return jnp.dot(a, b, preferred_element_type=F32)


def _dot_nt(a, b):
    return lax.dot_general(a, b, (((1,), (1,)), ((), ())), preferred_element_type=F32)


def _dot_f32(a, b):
    return jnp.dot(a, b, preferred_element_type=F32, precision=lax.Precision.HIGHEST)


def _params(sem):
    return pltpu.CompilerParams(dimension_semantics=sem, vmem_limit_bytes=VMEM_LIMIT)


def _full(shape):
    nd = len(shape)
    return pl.BlockSpec(shape, lambda *_: (0,) * nd)


def _proj_kernel(x_ref, nw_ref, wq_ref, wkv_ref, wz_ref, wxbc_ref, wdt_ref, wg_ref, bg_ref,
                 q_ref, kv_ref, z_ref, xbc_ref, dt_ref, g_ref):
    u = _rms(x_ref[...], nw_ref[...]).astype(BF)
    q_ref[...] = _dot(u, wq_ref[...]).astype(BF)
    kv_ref[...] = _dot(u, wkv_ref[...]).astype(BF)
    z_ref[...] = _dot(u, wz_ref[...]).astype(BF)
    xbc_ref[...] = _dot(u, wxbc_ref[...]).astype(BF)
    dt_ref[...] = _dot(u, wdt_ref[...])
    g_ref[...] = jax.nn.sigmoid(_dot(u, wg_ref[...]) + bg_ref[...]).astype(BF)


def _proj(x2, norm_w, w_in, w_gate, b_gate):
    n = x2.shape[0]
    tm = PROJ_TM
    s0, s1, s2, s3, s4 = Q_DIM, Q_DIM + KV_DIM, Q_DIM + 2 * KV_DIM, Q_DIM + 2 * KV_DIM + D_INNER, \
        Q_DIM + 2 * KV_DIM + D_INNER + XBC_DIM
    wq = (w_in[:, :s0] * (HEAD_DIM ** -0.5)).astype(BF)
    wk, wv = w_in[:, s0:s1], w_in[:, s1:s2]
    dup = lambda w: jnp.concatenate([w[:, :HEAD_DIM], w[:, :HEAD_DIM], w[:, HEAD_DIM:], w[:, HEAD_DIM:]], axis=1)
    wkv = jnp.concatenate([dup(wk), dup(wv)], axis=1).astype(BF)
    wz = w_in[:, s2:s3].astype(BF)
    wxbc = w_in[:, s3:s4].astype(BF)
    wdt = jnp.pad(w_in[:, s4:], ((0, 0), (0, LANES - SSM_HEADS))).astype(BF)
    wg = w_gate.astype(BF)
    outs = (
        jax.ShapeDtypeStruct((n, Q_DIM), BF),
        jax.ShapeDtypeStruct((n, 4 * LANES), BF),
        jax.ShapeDtypeStruct((n, D_INNER), BF),
        jax.ShapeDtypeStruct((n, XBC_DIM), BF),
        jax.ShapeDtypeStruct((n, LANES), F32),
        jax.ShapeDtypeStruct((n, 2 * D_MODEL), BF),
    )
    row = lambda w: pl.BlockSpec((tm, w), lambda i: (i, 0))
    return pl.pallas_call(
        _proj_kernel,
        out_shape=outs,
        grid=(n // tm,),
        in_specs=[row(D_MODEL), _full((1, D_MODEL)), _full(wq.shape), _full(wkv.shape), _full(wz.shape),
                  _full(wxbc.shape), _full(wdt.shape), _full(wg.shape), _full((1, 2 * D_MODEL))],
        out_specs=[row(Q_DIM), row(4 * LANES), row(D_INNER), row(XBC_DIM), row(LANES), row(2 * D_MODEL)],
        compiler_params=_params(("parallel",)),
        name="proj",
    )(x2, norm_w.reshape(1, -1), wq, wkv, wz, wxbc, wdt, wg, b_gate.reshape(1, -1))


PAIRS_PER_KV = ATTN_HEADS // ATTN_KV_HEADS // 2


def _swa_kernel(sink_ref, q_ref, kvc_ref, kvp_ref, bias_ref, o_ref):
    w = WINDOW
    upper = lax.broadcasted_iota(I32, (2 * w, LANES), 1) >= HEAD_DIM
    zero = jnp.zeros((2 * w, LANES), BF)
    ones = jnp.ones((2 * w, LANES), BF)
    chains = [(kvh, r) for kvh in range(ATTN_KV_HEADS) for r in range(2)]
    half = lambda t, r: jnp.where(upper, t, zero) if r else jnp.where(upper, zero, t)
    scores, values, sinks = [], [], []
    for kvh, r in chains:
        ks = slice(LANES * kvh, LANES * (kvh + 1))
        vs = slice(LANES * (ATTN_KV_HEADS + kvh), LANES * (ATTN_KV_HEADS + kvh + 1))
        k2 = jnp.concatenate([kvp_ref[:, ks], kvc_ref[:, ks]], axis=0)
        v2 = jnp.concatenate([kvp_ref[:, vs], kvc_ref[:, vs]], axis=0)
        q4 = jnp.concatenate([q_ref[:, LANES * (kvh * PAIRS_PER_KV + p):LANES * (kvh * PAIRS_PER_KV + p + 1)]
                              for p in range(PAIRS_PER_KV)], axis=0)
        scores.append(_dot_nt(q4, half(k2, r)) + bias_ref[0, kvh, r])
        values.append(jnp.concatenate([half(v2, r), ones], axis=1))
        sinks.append(jnp.concatenate(
            [jnp.full((w, 1), sink_ref[2 * (kvh * PAIRS_PER_KV + p) + r], F32) for p in range(PAIRS_PER_KV)],
            axis=0))
    maxes = [jnp.maximum(jnp.max(s, axis=-1, keepdims=True), sk) for s, sk in zip(scores, sinks)]
    probs = [jnp.exp(s - m).astype(BF) for s, m in zip(scores, maxes)]
    nds = [_dot(p, v) for p, v in zip(probs, values)]
    outs = [nd[:, :LANES] / (nd[:, LANES:] + jnp.exp(sk - m)) for nd, sk, m in zip(nds, sinks, maxes)]
    for kvh in range(ATTN_KV_HEADS):
        o4 = outs[2 * kvh] + outs[2 * kvh + 1]
        for p in range(PAIRS_PER_KV):
            pidx = kvh * PAIRS_PER_KV + p
            o_ref[:, LANES * pidx:LANES * (pidx + 1)] = o4[w * p:w * (p + 1)].astype(BF)


def _swa_bias():
    qi = jnp.arange(WINDOW)[:, None]
    kj = jnp.arange(2 * WINDOW)[None, :]
    dist = qi + WINDOW - kj
    in_window = (dist >= 0) & (dist < WINDOW)
    heads = jnp.arange(ATTN_HEADS).reshape(ATTN_KV_HEADS, PAIRS_PER_KV, 2).transpose(0, 2, 1)
    slopes = jnp.exp2(-8.0 * (heads + 1).astype(F32) / ATTN_HEADS)
    bias = -slopes[..., None, None] * dist.astype(F32)
    tabs = []
    for has_prev in (False, True):
        mask = in_window & (has_prev | (kj >= WINDOW))
        tabs.append(jnp.where(mask, bias, NEG_BIG).reshape(ATTN_KV_HEADS, 2, PAIRS_PER_KV * WINDOW, 2 * WINDOW))
    return jnp.stack(tabs)


def _swa(q, kv, sinks, bsz, seq):
    nb = seq // WINDOW
    n = bsz * seq
    return pl.pallas_call(
        _swa_kernel,
        out_shape=jax.ShapeDtypeStruct((n, Q_DIM), BF),
        grid=(bsz, nb),
        in_specs=[
            pl.BlockSpec(memory_space=pltpu.SMEM),
            pl.BlockSpec((WINDOW, Q_DIM), lambda b, j: (b * nb + j, 0)),
            pl.BlockSpec((WINDOW, 4 * LANES), lambda b, j: (b * nb + j, 0)),
            pl.BlockSpec((WINDOW, 4 * LANES), lambda b, j: (jnp.maximum(b * nb + j - 1, 0), 0)),
            pl.BlockSpec((1, ATTN_KV_HEADS, 2, PAIRS_PER_KV * WINDOW, 2 * WINDOW),
                         lambda b, j: (jnp.minimum(j, 1), 0, 0, 0, 0)),
        ],
        out_specs=pl.BlockSpec((WINDOW, Q_DIM), lambda b, j: (b * nb + j, 0)),
        compiler_params=_params(("parallel", "parallel")),
        name="swa",
    )(sinks.astype(F32), q, kv, kv, _swa_bias())


def _softplus(x):
    return jnp.maximum(x, 0.0) + jnp.log1p(jnp.exp(-jnp.abs(x)))


def _ssd_kernel(xbc_ref, z_ref, dtc_ref, dtr_ref, cw_ref, cb_ref, dtbc_ref, dtbr_ref, alc_ref, alr_ref,
                dskip_ref, nw_ref, o_ref, state_ref, tail_ref):
    c = pl.program_id(1)
    L = CHUNK

    @pl.when(c == 0)
    def _():
        state_ref[...] = jnp.zeros_like(state_ref)
        tail_ref[...] = jnp.zeros_like(tail_ref)

    x_raw = xbc_ref[...].astype(F32)
    prev8 = tail_ref[...]
    row8 = lax.broadcasted_iota(I32, (SUBLANES, XBC_DIM), 0)
    acc = x_raw * cw_ref[CONV_WIDTH - 1:CONV_WIDTH, :] + cb_ref[...]
    for s in range(1, CONV_WIDTH):
        sh = pltpu.roll(x_raw, s, axis=0)
        fix = pltpu.roll(prev8, s, axis=0)
        top = jnp.where(row8 < s, fix, sh[:SUBLANES])
        sh = jnp.concatenate([top, sh[SUBLANES:]], axis=0)
        acc = acc + sh * cw_ref[CONV_WIDTH - 1 - s:CONV_WIDTH - s, :]
    tail_ref[...] = x_raw[L - SUBLANES:, :]
    xc = acc * jax.nn.sigmoid(acc)

    dt_col = _softplus(dtc_ref[...] + dtbc_ref[...])
    dt_row = _softplus(dtr_ref[...] + dtbr_ref[...])
    a_col = dt_col * (-jnp.exp(alc_ref[...]))
    a_row = dt_row * (-jnp.exp(alr_ref[...]))
    ii = lax.broadcasted_iota(I32, (L, L), 0)
    jj = lax.broadcasted_iota(I32, (L, L), 1)
    causal = ii >= jj
    tri_l = jnp.where(causal, 1.0, 0.0).astype(F32)
    tri_u = jnp.where(ii <= jj, 1.0, 0.0).astype(F32)
    acum_col = _dot_f32(tri_l, a_col)
    acum_row = _dot_f32(a_row, tri_u)
    last_col = acum_col[L - 1:L, :]
    last_row = acum_row[:, L - 1:L]
    wrow = jnp.exp(last_row - acum_row) * dt_row

    upper = lax.broadcasted_iota(I32, (L, LANES), 1) >= 64
    upper1 = upper[:1]
    heads_per_group = SSM_HEADS // SSM_GROUPS
    group_w = D_INNER // SSM_GROUPS
    for g in range(SSM_GROUPS):
        b_g = xc[:, D_INNER + D_STATE * g:D_INNER + D_STATE * (g + 1)]
        c_g = xc[:, D_INNER + D_STATE * (SSM_GROUPS + g):D_INNER + D_STATE * (SSM_GROUPS + g + 1)]
        cb = _dot_nt(c_g.astype(BF), b_g.astype(BF))
        b_gt = b_g.T
        ys = []
        for j in range(heads_per_group // 2):
            pidx = g * (heads_per_group // 2) + j
            x_pair = xc[:, LANES * pidx:LANES * (pidx + 1)]
            st_pair = state_ref[pidx]
            acc_y = None
            acc_s = None
            for r in range(2):
                h = 2 * pidx + r
                acol = jnp.broadcast_to(acum_col[:, h:h + 1], (L, L))
                arow = acum_row[h:h + 1, :]
                lmat = jnp.exp(jnp.where(causal, acol - arow, NEG_BIG))
                m_h = cb * lmat * dt_row[h:h + 1, :]
                c_h = c_g * jnp.exp(acol)
                lhs = jnp.concatenate([m_h, c_h], axis=1).astype(BF)
                keep = upper if r == 1 else jnp.logical_not(upper)
                xm = jnp.where(keep, x_pair, 0.0).astype(BF)
                sm = jnp.where(keep, st_pair, 0.0).astype(BF)
                rhs = jnp.concatenate([xm, sm], axis=0)
                y = _dot(lhs, rhs)
                acc_y = y if acc_y is None else acc_y + y
                bw = (b_gt * wrow[h:h + 1, :]).astype(BF)
                sn = _dot(bw, xm)
                acc_s = sn if acc_s is None else acc_s + sn
            d0 = jnp.broadcast_to(jnp.exp(last_col[:, 2 * pidx:2 * pidx + 1]), (1, LANES))
            d1 = jnp.broadcast_to(jnp.exp(last_col[:, 2 * pidx + 1:2 * pidx + 2]), (1, LANES))
            state_ref[pidx] = st_pair * jnp.where(upper1, d1, d0) + acc_s
            ys.append(acc_y + x_pair * dskip_ref[:, LANES * pidx:LANES * (pidx + 1)])
        y_g = jnp.concatenate(ys, axis=1)
        z_g = z_ref[:, group_w * g:group_w * (g + 1)].astype(F32)
        y_g = y_g * (z_g * jax.nn.sigmoid(z_g))
        y_g = _rms(y_g, nw_ref[:, group_w * g:group_w * (g + 1)])
        o_ref[:, group_w * g:group_w * (g + 1)] = y_g.astype(BF)


def _ssd(xbc, z, dt_raw, conv_w, conv_b, dt_bias, a_log, d_skip, norm_w, bsz, seq):
    nc = seq // CHUNK
    n = bsz * seq
    pad = LANES - SSM_HEADS
    dt_row = dt_raw[:, :SSM_HEADS].T
    dtb_col = jnp.pad(dt_bias, (0, pad)).reshape(1, LANES)
    al_col = jnp.pad(a_log, (0, pad)).reshape(1, LANES)
    dtb_row = jnp.broadcast_to(dt_bias[:, None], (SSM_HEADS, CHUNK))
    al_row = jnp.broadcast_to(a_log[:, None], (SSM_HEADS, CHUNK))
    dskip = jnp.repeat(d_skip, D_INNER // SSM_HEADS).reshape(1, D_INNER)
    blk = lambda w: pl.BlockSpec((CHUNK, w), lambda b, j: (b * nc + j, 0))
    return pl.pallas_call(
        _ssd_kernel,
        out_shape=jax.ShapeDtypeStruct((n, D_INNER), BF),
        grid=(bsz, nc),
        in_specs=[
            blk(XBC_DIM), blk(D_INNER), blk(LANES),
            pl.BlockSpec((SSM_HEADS, CHUNK), lambda b, j: (0, b * nc + j)),
            _full((CONV_WIDTH, XBC_DIM)), _full((1, XBC_DIM)),
            _full((1, LANES)), _full((SSM_HEADS, CHUNK)), _full((1, LANES)), _full((SSM_HEADS, CHUNK)),
            _full((1, D_INNER)), _full((1, D_INNER)),
        ],
        out_specs=blk(D_INNER),
        scratch_shapes=[pltpu.VMEM((SSM_HEADS // 2, D_STATE, LANES), F32),
                        pltpu.VMEM((SUBLANES, XBC_DIM), F32)],
        compiler_params=_params(("parallel", "arbitrary")),
        name="ssd",
    )(xbc, z, dt_raw, dt_row, conv_w, conv_b.reshape(1, -1), dtb_col, dtb_row, al_col, al_row,
      dskip, norm_w.reshape(1, -1))


def _outproj_kernel(x_ref, attn_ref, ssm_ref, g_ref, wa_ref, ws_ref, wo_ref, h_ref):
    a = _dot(attn_ref[...], wa_ref[...])
    s = _dot(ssm_ref[...], ws_ref[...])
    merged = g_ref[:, :D_MODEL].astype(F32) * a + g_ref[:, D_MODEL:].astype(F32) * s
    h_ref[...] = x_ref[...] + _dot(merged.astype(BF), wo_ref[...])


def _outproj(x2, attn, ssm, gates, w_attn_o, w_ssm_o, w_out):
    n = x2.shape[0]
    tm = OUT_TM
    row = lambda w: pl.BlockSpec((tm, w), lambda i: (i, 0))
    return pl.pallas_call(
        _outproj_kernel,
        out_shape=jax.ShapeDtypeStruct((n, D_MODEL), F32),
        grid=(n // tm,),
        in_specs=[row(D_MODEL), row(Q_DIM), row(D_INNER), row(2 * D_MODEL),
                  _full((Q_DIM, D_MODEL)), _full((D_INNER, D_MODEL)), _full((D_MODEL, D_MODEL))],
        out_specs=row(D_MODEL),
        compiler_params=_params(("parallel",)),
        name="outproj",
    )(x2, attn, ssm, gates, w_attn_o.astype(BF), w_ssm_o.astype(BF), w_out.astype(BF))


def _memkv_kernel(mem_ref, nw_ref, w_ref, kv_ref):
    mn = _rms(mem_ref[...], nw_ref[...]).astype(BF)
    kv_ref[...] = _dot(mn, w_ref[...]).astype(BF)


def _memkv(mem2, norm_w, w_ckv, bsz):
    return pl.pallas_call(
        _memkv_kernel,
        out_shape=jax.ShapeDtypeStruct((bsz * MEM_TOKENS, 2 * D_MODEL), BF),
        grid=(bsz,),
        in_specs=[pl.BlockSpec((MEM_TOKENS, D_MODEL), lambda b: (b, 0)), _full((1, D_MODEL)),
                  _full((D_MODEL, 2 * D_MODEL))],
        out_specs=pl.BlockSpec((MEM_TOKENS, 2 * D_MODEL), lambda b: (b, 0)),
        compiler_params=_params(("parallel",)),
        name="memkv",
    )(mem2, norm_w.reshape(1, -1), w_ckv.astype(BF))


def _cross_kernel(h_ref, kv_ref, nw_ref, wq_ref, wo_ref, o_ref):
    h = h_ref[...]
    hn = _rms(h, nw_ref[...]).astype(BF)
    q = (_dot(hn, wq_ref[...]) * (CROSS_HEAD_DIM ** -0.5)).astype(BF)
    outs = []
    for hd in range(CROSS_HEADS):
        lo, hi = CROSS_HEAD_DIM * hd, CROSS_HEAD_DIM * (hd + 1)
        s = _dot_nt(q[:, lo:hi], kv_ref[:, lo:hi])
        p = jnp.exp(s - jnp.max(s, axis=-1, keepdims=True))
        den = jnp.sum(p, axis=-1, keepdims=True)
        outs.append((_dot(p.astype(BF), kv_ref[:, D_MODEL + lo:D_MODEL + hi]) / den).astype(BF))
    o = jnp.concatenate(outs, axis=1)
    o_ref[...] = h + _dot(o, wo_ref[...])


def _cross(h, kv, norm_w, w_cq, w_co, seq):
    n = h.shape[0]
    tm = CROSS_TM
    per_b = seq // tm
    row = pl.BlockSpec((tm, D_MODEL), lambda i: (i, 0))
    return pl.pallas_call(
        _cross_kernel,
        out_shape=jax.ShapeDtypeStruct((n, D_MODEL), F32),
        grid=(n // tm,),
        in_specs=[row, pl.BlockSpec((MEM_TOKENS, 2 * D_MODEL), lambda i: (i // per_b, 0)),
                  _full((1, D_MODEL)), _full((D_MODEL, D_MODEL)), _full((D_MODEL, D_MODEL))],
        out_specs=row,
        compiler_params=_params(("parallel",)),
        name="cross",
    )(h, kv, norm_w.reshape(1, -1), w_cq.astype(BF), w_co.astype(BF))


def _router_kernel(h_ref, nw_ref, wr_ref, br_ref, t_ref, meta_ref, cnt_ref, carry_ref):
    i = pl.program_id(0)
    tm = ROUTER_TM

    @pl.when(i == 0)
    def _():
        carry_ref[...] = jnp.zeros_like(carry_ref)

    t = _rms(h_ref[...], nw_ref[...])
    for c in range(ROW_TILES):
        t_ref[pl.ds(c, tm, stride=ROW_TILES), :] = t[:, LANES * c:LANES * (c + 1)]
    lane = lax.broadcasted_iota(I32, (tm, LANES), 1)
    logits = jnp.where(lane < N_EXPERTS, _dot_f32(t, wr_ref[...]) + br_ref[...], NEG_BIG)
    work = logits
    vals, ids, hots = [], [], []
    for _k in range(TOP_K):
        m = jnp.max(work, axis=-1, keepdims=True)
        idx = jnp.min(jnp.where(work == m, lane, LANES), axis=-1, keepdims=True)
        hot = lane == idx
        vals.append(m)
        ids.append(idx)
        hots.append(hot)
        work = jnp.where(hot, NEG_BIG * 2.0, work)
    es = [jnp.exp(v - vals[0]) for v in vals]
    den = es[0] + es[1] + es[2] + es[3]
    member = jnp.zeros((tm, LANES), F32)
    for hot in hots:
        member = member + jnp.where(hot, 1.0, 0.0)
    ii = lax.broadcasted_iota(I32, (tm, tm), 0)
    jj = lax.broadcasted_iota(I32, (tm, tm), 1)
    strict = jnp.where(ii > jj, 1.0, 0.0).astype(BF)
    carry = carry_ref[0:1, :]
    before = _dot(strict, member.astype(BF)) + carry
    meta = jnp.zeros((tm, LANES), F32)
    for k in range(TOP_K):
        rank = jnp.sum(jnp.where(hots[k], before, 0.0), axis=-1, keepdims=True)
        meta = jnp.where(lane == k, ids[k].astype(F32), meta)
        meta = jnp.where(lane == TOP_K + k, rank, meta)
        meta = jnp.where(lane == 2 * TOP_K + k, es[k] / den, meta)
    meta_ref[...] = meta
    new_carry = carry + jnp.sum(member, axis=0, keepdims=True)
    carry_ref[...] = jnp.broadcast_to(new_carry, carry_ref.shape)
    cnt_ref[...] = jnp.broadcast_to(new_carry, cnt_ref.shape)


def _router(h, norm_w, w_router, b_router):
    n = h.shape[0]
    tm = ROUTER_TM
    wr = jnp.pad(w_router, ((0, 0), (0, LANES - N_EXPERTS)))
    br = jnp.pad(b_router, (0, LANES - N_EXPERTS)).reshape(1, LANES)
    return pl.pallas_call(
        _router_kernel,
        out_shape=(jax.ShapeDtypeStruct((n * ROW_TILES, LANES), F32),
                   jax.ShapeDtypeStruct((n, LANES), F32),
                   jax.ShapeDtypeStruct((SUBLANES, LANES), F32)),
        grid=(n // tm,),
        in_specs=[pl.BlockSpec((tm, D_MODEL), lambda i: (i, 0)), _full((1, D_MODEL)),
                  _full((D_MODEL, LANES)), _full((1, LANES))],
        out_specs=[pl.BlockSpec((tm * ROW_TILES, LANES), lambda i: (i, 0)),
                   pl.BlockSpec((tm, LANES), lambda i: (i, 0)),
                   _full((SUBLANES, LANES))],
        scratch_shapes=[pltpu.VMEM((SUBLANES, LANES), F32)],
        compiler_params=_params(("arbitrary",)),
        name="router",
    )(h, norm_w.reshape(1, -1), wr, br)


def _row(ref, r):
    return ref.at[pl.ds(pl.multiple_of(r * ROW_TILES, ROW_TILES), ROW_TILES)]


def _dispatch_kernel(dest_ref, t_ref, xs_hbm, sem):
    tt = DISPATCH_TT

    def copy(j, k):
        return pltpu.make_async_copy(_row(t_ref, j), _row(xs_hbm, dest_ref[j * TOP_K + k]), sem)

    def issue(j, carry):
        for k in range(TOP_K):
            copy(j, k).start()
        return carry

    def drain(j, carry):
        for k in range(TOP_K):
            copy(j, k).wait()
        return carry

    lax.fori_loop(0, tt, issue, 0)
    lax.fori_loop(0, tt, drain, 0)


def _dispatch(t_rows, dest):
    n = t_rows.shape[0] // ROW_TILES
    tt = DISPATCH_TT
    return pl.pallas_call(
        _dispatch_kernel,
        out_shape=jax.ShapeDtypeStruct((n * TOP_K * ROW_TILES, LANES), F32),
        grid=(n // tt,),
        in_specs=[pl.BlockSpec((tt * TOP_K,), lambda i: (i,), memory_space=pltpu.SMEM),
                  pl.BlockSpec((tt * ROW_TILES, LANES), lambda i: (i, 0))],
        out_specs=pl.BlockSpec(memory_space=pl.ANY),
        scratch_shapes=[pltpu.SemaphoreType.DMA],
        compiler_params=_params(("arbitrary",)),
        name="dispatch",
    )(dest, t_rows)


def _expert_kernel(tile_ref, exp_ref, lo_ref, hi_ref, first_ref, newexp_ref, xs_ref, wgu_ref, bgu_ref, wdn_ref,
                   bdn_ref, ys_ref, wgu_bf, wdn_bf):
    w = pl.program_id(0)
    tm = EXPERT_TM
    lo = lo_ref[w]
    hi = hi_ref[w]

    @pl.when(newexp_ref[w] == 1)
    def _():
        wgu_bf[...] = wgu_ref[0].astype(BF)
        wdn_bf[...] = wdn_ref[0].astype(BF)

    @pl.when(hi > lo)
    def _():
        x = jnp.concatenate([xs_ref[pl.ds(c, tm, stride=ROW_TILES), :] for c in range(ROW_TILES)],
                            axis=1).astype(BF)
        gu = _dot(x, wgu_bf[...]) + bgu_ref[0]
        g = jnp.minimum(gu[:, :D_EXPERT], SWIGLU_LIMIT)
        up = jnp.clip(gu[:, D_EXPERT:], -SWIGLU_LIMIT, SWIGLU_LIMIT)
        act = (up + 1.0) * (g * jax.nn.sigmoid(SWIGLU_ALPHA * g))
        y = _dot(act.astype(BF), wdn_bf[...]) + bdn_ref[0]
        rows = tile_ref[w] * tm + lax.broadcasted_iota(I32, (tm, LANES), 0)
        mine = (rows >= lo) & (rows < hi)
        is_first = first_ref[w] == 1

        @pl.when(is_first)
        def _():
            for c in range(ROW_TILES):
                ys_ref[pl.ds(c, tm, stride=ROW_TILES), :] = jnp.where(mine, y[:, LANES * c:LANES * (c + 1)], 0.0)

        @pl.when(jnp.logical_not(is_first))
        def _():
            for c in range(ROW_TILES):
                old = ys_ref[pl.ds(c, tm, stride=ROW_TILES), :]
                ys_ref[pl.ds(c, tm, stride=ROW_TILES), :] = jnp.where(mine, y[:, LANES * c:LANES * (c + 1)], old)


def _experts(xs, items, w_gu, b_gu, w_dn, b_dn):
    tile_of, exp_of, lo, hi, first, newexp = items
    n_items = tile_of.shape[0]
    tm = EXPERT_TM
    grid_spec = pltpu.PrefetchScalarGridSpec(
        num_scalar_prefetch=6,
        grid=(n_items,),
        in_specs=[
            pl.BlockSpec((tm * ROW_TILES, LANES), lambda w, t, e, *_: (t[w], 0)),
            pl.BlockSpec((1, D_MODEL, 2 * D_EXPERT), lambda w, t, e, *_: (e[w], 0, 0)),
            pl.BlockSpec((1, 1, 2 * D_EXPERT), lambda w, t, e, *_: (e[w], 0, 0)),
            pl.BlockSpec((1, D_EXPERT, D_MODEL), lambda w, t, e, *_: (e[w], 0, 0)),
            pl.BlockSpec((1, 1, D_MODEL), lambda w, t, e, *_: (e[w], 0, 0)),
        ],
        out_specs=pl.BlockSpec((tm * ROW_TILES, LANES), lambda w, t, e, *_: (t[w], 0)),
        scratch_shapes=[pltpu.VMEM((D_MODEL, 2 * D_EXPERT), BF), pltpu.VMEM((D_EXPERT, D_MODEL), BF)],
    )
    return pl.pallas_call(
        _expert_kernel,
        out_shape=jax.ShapeDtypeStruct(xs.shape, F32),
        grid_spec=grid_spec,
        compiler_params=_params(("arbitrary",)),
        name="experts",
    )(tile_of, exp_of, lo, hi, first, newexp, xs, w_gu, b_gu.reshape(N_EXPERTS, 1, -1),
      w_dn, b_dn.reshape(N_EXPERTS, 1, -1))


def _expert_items(counts, n_rows):
    tm = EXPERT_TM
    n_tiles = n_rows // tm
    n_items = n_tiles + N_EXPERTS - 1
    end = jnp.cumsum(counts)
    off = end - counts
    first_tile = off // tm
    last_tile = jnp.maximum(end - 1, 0) // tm
    per = jnp.where(counts > 0, last_tile - first_tile + 1, 0)
    cum = jnp.cumsum(per)
    start = cum - per
    total = cum[-1]
    w = jnp.arange(n_items, dtype=I32)
    wc = jnp.minimum(w, total - 1)
    e_of = jnp.minimum(jnp.sum((cum[None, :] <= wc[:, None]).astype(I32), axis=1), N_EXPERTS - 1)
    sel = e_of[:, None] == jnp.arange(N_EXPERTS, dtype=I32)
    pick = lambda v: jnp.sum(jnp.where(sel, v[None, :], 0), axis=1)
    t_of = (pick(first_tile) + (wc - pick(start))).astype(I32)
    valid = w < total
    lo = jnp.where(valid, jnp.maximum(pick(off), t_of * tm), 0).astype(I32)
    hi = jnp.where(valid, jnp.minimum(pick(end), (t_of + 1) * tm), 0).astype(I32)
    prev_t = jnp.concatenate([jnp.full((1,), -1, I32), t_of[:-1]])
    first = (t_of != prev_t).astype(I32)
    prev_e = jnp.concatenate([jnp.full((1,), -1, I32), e_of[:-1]])
    newexp = (e_of != prev_e).astype(I32)
    return t_of, e_of, lo, hi, first, newexp


def _combine_kernel(dest_ref, meta_ref, h_ref, fw_ref, ys_hbm, o_ref, buf_ref, sem):
    tt = COMBINE_TT

    def copy(j, k):
        return pltpu.make_async_copy(_row(ys_hbm, dest_ref[j * TOP_K + k]), _row(buf_ref, k * tt + j), sem)

    def issue(j, carry):
        for k in range(TOP_K):
            copy(j, k).start()
        return carry

    def drain(j, carry):
        for k in range(TOP_K):
            copy(j, k).wait()
        return carry

    lax.fori_loop(0, tt, issue, 0)
    lax.fori_loop(0, tt, drain, 0)
    meta = meta_ref[...]
    wts = [jnp.broadcast_to(meta[:, 2 * TOP_K + k:2 * TOP_K + k + 1], (tt, LANES)) for k in range(TOP_K)]
    cols = []
    for c in range(ROW_TILES):
        acc = h_ref[:, LANES * c:LANES * (c + 1)]
        for k in range(TOP_K):
            acc = acc + wts[k] * buf_ref[pl.ds(k * tt * ROW_TILES + c, tt, stride=ROW_TILES), :]
        cols.append(acc)
    o_ref[...] = _rms(jnp.concatenate(cols, axis=1), fw_ref[...])


def _combine(dest, meta, h, final_w, ys):
    n = h.shape[0]
    tt = COMBINE_TT
    return pl.pallas_call(
        _combine_kernel,
        out_shape=jax.ShapeDtypeStruct((n, D_MODEL), F32),
        grid=(n // tt,),
        in_specs=[pl.BlockSpec((tt * TOP_K,), lambda i: (i,), memory_space=pltpu.SMEM),
                  pl.BlockSpec((tt, LANES), lambda i: (i, 0)),
                  pl.BlockSpec((tt, D_MODEL), lambda i: (i, 0)),
                  _full((1, D_MODEL)),
                  pl.BlockSpec(memory_space=pl.ANY)],
        out_specs=pl.BlockSpec((tt, D_MODEL), lambda i: (i, 0)),
        scratch_shapes=[pltpu.VMEM((TOP_K * tt * ROW_TILES, LANES), F32), pltpu.SemaphoreType.DMA],
        compiler_params=_params(("arbitrary",)),
        name="combine",
    )(dest, meta, h, final_w.reshape(1, -1), ys)


def _moe(h, norm_w, w_router, b_router, w_gu, b_gu, w_dn, b_dn, final_w):
    n = h.shape[0]
    t_rows, meta, cnt = _router(h, norm_w, w_router, b_router)
    eid = meta[:, :TOP_K].astype(I32)
    rank = meta[:, TOP_K:2 * TOP_K].astype(I32)
    counts = cnt[0, :N_EXPERTS].astype(I32)
    off = jnp.cumsum(counts) - counts
    hot = eid[..., None] == jnp.arange(N_EXPERTS, dtype=I32)
    dest = (jnp.sum(jnp.where(hot, off, 0), axis=-1) + rank).reshape(-1)
    xs = _dispatch(t_rows, dest)
    ys = _experts(xs, _expert_items(counts, n * TOP_K), w_gu, b_gu, w_dn, b_dn)
    return _combine(dest, meta, h, final_w, ys)


def kernel(x, mem, norm_mix_w, w_in, w_gate, b_gate, attn_sinks, conv_w, conv_b, dt_bias, a_log, d_skip,
           ssm_norm_w, w_attn_o, w_ssm_o, w_out, norm_cross_w, norm_mem_w, w_cq, w_ckv, w_co,
           norm_ffn_w, w_router, b_router, w_gu, b_gu, w_dn, b_dn, final_norm_w):
    bsz, seq, _ = x.shape
    assert norm_mix_w.shape[0] == 1, "single-layer block"
    x2 = x.reshape(bsz * seq, D_MODEL)
    q, kv, z, xbc, dt_raw, gates = _proj(x2, norm_mix_w[0], w_in[0], w_gate[0], b_gate[0])
    attn = _swa(q, kv, attn_sinks[0], bsz, seq)
    ssm = _ssd(xbc, z, dt_raw, conv_w[0], conv_b[0], dt_bias[0], a_log[0], d_skip[0], ssm_norm_w[0], bsz, seq)
    h = _outproj(x2, attn, ssm, gates, w_attn_o[0], w_ssm_o[0], w_out[0])
    mkv = _memkv(mem.reshape(bsz * MEM_TOKENS, D_MODEL), norm_mem_w[0], w_ckv[0], bsz)
    h = _cross(h, mkv, norm_cross_w[0], w_cq[0], w_co[0], seq)
    out = _moe(h, norm_ffn_w[0], w_router[0], b_router[0], w_gu[0], b_gu[0], w_dn[0], b_dn[0], final_norm_w)
    return out.reshape(bsz, seq, D_MODEL)
```

```python
import functools

import jax
import jax.numpy as jnp
from jax import lax
from jax.experimental import pallas as pl
from jax.experimental.pallas import tpu as pltpu

BF = jnp.bfloat16
F32 = jnp.float32
I32 = jnp.int32

D_MODEL = 1024
RMS_EPS = 1e-5
ATTN_HEADS = 16
ATTN_KV_HEADS = 2
HEAD_DIM = 64
WINDOW = 128
Q_DIM = ATTN_HEADS * HEAD_DIM
KV_DIM = ATTN_KV_HEADS * HEAD_DIM
D_INNER = 2 * D_MODEL
SSM_HEADS = 32
SSM_GROUPS = 4
D_STATE = 128
CONV_WIDTH = 4
CHUNK = 128
XBC_DIM = D_INNER + 2 * SSM_GROUPS * D_STATE
MEM_TOKENS = 256
CROSS_HEADS = 4
CROSS_HEAD_DIM = D_MODEL // CROSS_HEADS
N_EXPERTS = 32
TOP_K = 4
D_EXPERT = D_MODEL
SWIGLU_LIMIT = 7.0
SWIGLU_ALPHA = 1.702

LANES = 128
SUBLANES = 8
ROW_TILES = D_MODEL // LANES
NEG_BIG = -1e30
VMEM_LIMIT = 56 * 1024 * 1024

PROJ_TM = 256
CONV_COLS = 512
OUT_TM = 512
CROSS_TM = 512
ROUTER_TM = 512
DISPATCH_TT = 512
EXPERT_TM = 256
COMBINE_TT = 512
COMBINE_PHASES = 2


def _rms(x, w):
    return x * lax.rsqrt(jnp.mean(x * x, axis=-1, keepdims=True) + RMS_EPS) * w


def _dot(a, b):
    return jnp.dot(a, b, preferred_element_type=F32)


def _dot_nt(a, b):
    return lax.dot_general(a, b, (((1,), (1,)), ((), ())), preferred_element_type=F32)


def _dot_f32(a, b):
    return jnp.dot(a, b, preferred_element_type=F32, precision=lax.Precision.HIGHEST)


def _params(sem):
    return pltpu.CompilerParams(dimension_semantics=sem, vmem_limit_bytes=VMEM_LIMIT)


def _full(shape):
    nd = len(shape)
    return pl.BlockSpec(shape, lambda *_: (0,) * nd)


def _proj_kernel(tiles_per_seq, x_ref, nw_ref, wq_ref, wkv_ref, wz_ref, wxbc_ref, wdt_ref, wg_ref, bg_ref,
                 cw_ref, cb_ref, q_ref, kv_ref, z_ref, xc_ref, dt_ref, g_ref, tail_ref):
    tm = PROJ_TM

    @pl.when(pl.program_id(0) % tiles_per_seq == 0)
    def _():
        tail_ref[...] = jnp.zeros_like(tail_ref)

    u = _rms(x_ref[...], nw_ref[...]).astype(BF)
    row8 = lax.broadcasted_iota(I32, (SUBLANES, CONV_COLS), 0)

    def conv_block(j):
        cols = slice(CONV_COLS * j, CONV_COLS * (j + 1))
        xbc = _dot(u, wxbc_ref[:, cols])
        prev8 = tail_ref[:, cols]
        acc = xbc * cw_ref[CONV_WIDTH - 1:CONV_WIDTH, cols] + cb_ref[:, cols]
        for s in range(1, CONV_WIDTH):
            sh = pltpu.roll(xbc, s, axis=0)
            top = jnp.where(row8 < s, pltpu.roll(prev8, s, axis=0), sh[:SUBLANES])
            sh = jnp.concatenate([top, sh[SUBLANES:]], axis=0)
            acc = acc + sh * cw_ref[CONV_WIDTH - 1 - s:CONV_WIDTH - s, cols]
        tail_ref[:, cols] = xbc[tm - SUBLANES:, :]
        xc_ref[:, cols] = (acc * jax.nn.sigmoid(acc)).astype(BF)

    def gate_half(k):
        cols = slice(D_MODEL * k, D_MODEL * (k + 1))
        g_ref[:, cols] = jax.nn.sigmoid(_dot(u, wg_ref[:, cols]) + bg_ref[:, cols]).astype(BF)

    def z_half(k):
        cols = slice(D_MODEL * k, D_MODEL * (k + 1))
        z_ref[:, cols] = _dot(u, wz_ref[:, cols]).astype(BF)

    def q_proj():
        q_ref[...] = _dot(u, wq_ref[...]).astype(BF)

    def kv_dt_proj():
        kv_ref[...] = _dot(u, wkv_ref[...]).astype(BF)
        dt_ref[...] = _dot(u, wdt_ref[...])

    others = [q_proj, kv_dt_proj, lambda: z_half(0), lambda: z_half(1), lambda: gate_half(0), lambda: gate_half(1)]
    for j in range(XBC_DIM // CONV_COLS):
        others[j]()
        conv_block(j)


def _proj(x2, norm_w, w_in, w_gate, b_gate, conv_w, conv_b, seq):
    n = x2.shape[0]
    tm = PROJ_TM
    s0, s1, s2, s3, s4 = Q_DIM, Q_DIM + KV_DIM, Q_DIM + 2 * KV_DIM, Q_DIM + 2 * KV_DIM + D_INNER, \
        Q_DIM + 2 * KV_DIM + D_INNER + XBC_DIM
    wq = (w_in[:, :s0] * (HEAD_DIM ** -0.5)).astype(BF)
    wk, wv = w_in[:, s0:s1], w_in[:, s1:s2]
    dup = lambda w: jnp.concatenate([w[:, :HEAD_DIM], w[:, :HEAD_DIM], w[:, HEAD_DIM:], w[:, HEAD_DIM:]], axis=1)
    wkv = jnp.concatenate([dup(wk), dup(wv)], axis=1).astype(BF)
    wz = w_in[:, s2:s3].astype(BF)
    wxbc = w_in[:, s3:s4].astype(BF)
    wdt = jnp.pad(w_in[:, s4:], ((0, 0), (0, LANES - SSM_HEADS))).astype(BF)
    wg = w_gate.astype(BF)
    outs = (
        jax.ShapeDtypeStruct((n, Q_DIM), BF),
        jax.ShapeDtypeStruct((n, 4 * LANES), BF),
        jax.ShapeDtypeStruct((n, D_INNER), BF),
        jax.ShapeDtypeStruct((n, XBC_DIM), BF),
        jax.ShapeDtypeStruct((n, LANES), F32),
        jax.ShapeDtypeStruct((n, 2 * D_MODEL), BF),
    )
    row = lambda w: pl.BlockSpec((tm, w), lambda i: (i, 0))
    return pl.pallas_call(
        functools.partial(_proj_kernel, seq // tm),
        out_shape=outs,
        grid=(n // tm,),
        in_specs=[row(D_MODEL), _full((1, D_MODEL)), _full(wq.shape), _full(wkv.shape), _full(wz.shape),
                  _full(wxbc.shape), _full(wdt.shape), _full(wg.shape), _full((1, 2 * D_MODEL)),
                  _full((CONV_WIDTH, XBC_DIM)), _full((1, XBC_DIM))],
        out_specs=[row(Q_DIM), row(4 * LANES), row(D_INNER), row(XBC_DIM), row(LANES), row(2 * D_MODEL)],
        scratch_shapes=[pltpu.VMEM((SUBLANES, XBC_DIM), F32)],
        compiler_params=_params(("arbitrary",)),
        name="proj",
    )(x2, norm_w.reshape(1, -1), wq, wkv, wz, wxbc, wdt, wg, b_gate.reshape(1, -1), conv_w,
      conv_b.reshape(1, -1))


PAIRS_PER_KV = ATTN_HEADS // ATTN_KV_HEADS // 2


SWA_BLOCKS = 2


def _swa_block(sink_ref, q_ref, kv_prev, kv_cur, bias, o_ref):
    w = WINDOW
    upper = lax.broadcasted_iota(I32, (2 * w, LANES), 1) >= HEAD_DIM
    zero = jnp.zeros((2 * w, LANES), BF)
    ones = jnp.ones((2 * w, LANES), BF)
    chains = [(kvh, r) for kvh in range(ATTN_KV_HEADS) for r in range(2)]
    half = lambda t, r: jnp.where(upper, t, zero) if r else jnp.where(upper, zero, t)
    scores, values, sinks = [], [], []
    for kvh, r in chains:
        ks = slice(LANES * kvh, LANES * (kvh + 1))
        vs = slice(LANES * (ATTN_KV_HEADS + kvh), LANES * (ATTN_KV_HEADS + kvh + 1))
        k2 = jnp.concatenate([kv_prev[:, ks], kv_cur[:, ks]], axis=0)
        v2 = jnp.concatenate([kv_prev[:, vs], kv_cur[:, vs]], axis=0)
        q4 = jnp.concatenate([q_ref[:, LANES * (kvh * PAIRS_PER_KV + p):LANES * (kvh * PAIRS_PER_KV + p + 1)]
                              for p in range(PAIRS_PER_KV)], axis=0)
        scores.append(_dot_nt(q4, half(k2, r)) + bias(kvh, r))
        values.append(jnp.concatenate([half(v2, r), ones], axis=1))
        sinks.append(jnp.concatenate(
            [jnp.full((w, 1), sink_ref[2 * (kvh * PAIRS_PER_KV + p) + r], F32) for p in range(PAIRS_PER_KV)],
            axis=0))
    maxes = [jnp.maximum(jnp.max(s, axis=-1, keepdims=True), sk) for s, sk in zip(scores, sinks)]
    probs = [jnp.exp(s - m).astype(BF) for s, m in zip(scores, maxes)]
    nds = [_dot(p, v) for p, v in zip(probs, values)]
    outs = [nd[:, :LANES] / (nd[:, LANES:] + jnp.exp(sk - m)) for nd, sk, m in zip(nds, sinks, maxes)]
    for kvh in range(ATTN_KV_HEADS):
        o4 = outs[2 * kvh] + outs[2 * kvh + 1]
        for p in range(PAIRS_PER_KV):
            pidx = kvh * PAIRS_PER_KV + p
            o_ref[:, LANES * pidx:LANES * (pidx + 1)] = o4[w * p:w * (p + 1)].astype(BF)


def _swa_kernel(sink_ref, q_ref, kvc_ref, kvp_ref, bias_ref, o_ref):
    has_prev = jnp.minimum(pl.program_id(1), 1)
    for sub in range(SWA_BLOCKS):
        rows = pl.ds(WINDOW * sub, WINDOW)
        kv_prev = kvp_ref if sub == 0 else kvc_ref.at[pl.ds(WINDOW * (sub - 1), WINDOW)]
        variant = has_prev if sub == 0 else 1
        _swa_block(sink_ref, q_ref.at[rows], kv_prev, kvc_ref.at[rows],
                   lambda kvh, r, v=variant: bias_ref[v, kvh, r], o_ref.at[rows])


def _swa_bias():
    qi = jnp.arange(WINDOW)[:, None]
    kj = jnp.arange(2 * WINDOW)[None, :]
    dist = qi + WINDOW - kj
    in_window = (dist >= 0) & (dist < WINDOW)
    heads = jnp.arange(ATTN_HEADS).reshape(ATTN_KV_HEADS, PAIRS_PER_KV, 2).transpose(0, 2, 1)
    slopes = jnp.exp2(-8.0 * (heads + 1).astype(F32) / ATTN_HEADS)
    bias = -slopes[..., None, None] * dist.astype(F32)
    tabs = []
    for has_prev in (False, True):
        mask = in_window & (has_prev | (kj >= WINDOW))
        tabs.append(jnp.where(mask, bias, NEG_BIG).reshape(ATTN_KV_HEADS, 2, PAIRS_PER_KV * WINDOW, 2 * WINDOW))
    return jnp.stack(tabs)


def _swa(q, kv, sinks, bsz, seq):
    nb = seq // (WINDOW * SWA_BLOCKS)
    n = bsz * seq
    rows = WINDOW * SWA_BLOCKS
    return pl.pallas_call(
        _swa_kernel,
        out_shape=jax.ShapeDtypeStruct((n, Q_DIM), BF),
        grid=(bsz, nb),
        in_specs=[
            pl.BlockSpec(memory_space=pltpu.SMEM),
            pl.BlockSpec((rows, Q_DIM), lambda b, j: (b * nb + j, 0)),
            pl.BlockSpec((rows, 4 * LANES), lambda b, j: (b * nb + j, 0)),
            pl.BlockSpec((WINDOW, 4 * LANES), lambda b, j: (jnp.maximum((b * nb + j) * SWA_BLOCKS - 1, 0), 0)),
            _full((2, ATTN_KV_HEADS, 2, PAIRS_PER_KV * WINDOW, 2 * WINDOW)),
        ],
        out_specs=pl.BlockSpec((rows, Q_DIM), lambda b, j: (b * nb + j, 0)),
        compiler_params=_params(("parallel", "parallel")),
        name="swa",
    )(sinks.astype(F32), q, kv, kv, _swa_bias())


def _softplus(x):
    return jnp.maximum(x, 0.0) + jnp.log1p(jnp.exp(-jnp.abs(x)))


SSD_CHUNKS = 2


def _ssd_chunk(xc_ref, z_ref, dtc_ref, dt_row_raw, dtbc_ref, dtbr_ref, alc_ref, alr_ref,
               dskip_ref, nw_ref, o_ref, state_ref):
    L = CHUNK
    xc = xc_ref[...]

    dt_col = _softplus(dtc_ref[...] + dtbc_ref[...])
    dt_row = _softplus(dt_row_raw + dtbr_ref[...])
    a_col = dt_col * (-jnp.exp(alc_ref[...]))
    a_row = dt_row * (-jnp.exp(alr_ref[...]))
    ii = lax.broadcasted_iota(I32, (L, L), 0)
    jj = lax.broadcasted_iota(I32, (L, L), 1)
    causal = ii >= jj
    tri_l = jnp.where(causal, 1.0, 0.0).astype(F32)
    tri_u = jnp.where(ii <= jj, 1.0, 0.0).astype(F32)
    acum_col = _dot_f32(tri_l, a_col)
    acum_row = _dot_f32(a_row, tri_u)
    decay_col = jnp.exp(acum_col)
    last_col = decay_col[L - 1:L, :]
    last_row = acum_row[:, L - 1:L]
    wrow = jnp.exp(last_row - acum_row) * dt_row
    shifted_row = acum_row - jnp.log(dt_row)

    upper = lax.broadcasted_iota(I32, (L, LANES), 1) >= 64
    upper1 = upper[:1]
    lower = jnp.logical_not(upper)
    pairs_per_group = SSM_HEADS // SSM_GROUPS // 2
    group_w = D_INNER // SSM_GROUPS
    for g in range(SSM_GROUPS):
        b_g = xc[:, D_INNER + D_STATE * g:D_INNER + D_STATE * (g + 1)]
        c_g = xc[:, D_INNER + D_STATE * (SSM_GROUPS + g):D_INNER + D_STATE * (SSM_GROUPS + g + 1)]
        c_bf = c_g.astype(BF)
        cb = _dot_nt(c_bf, b_g.astype(BF))
        b_gt = b_g.astype(F32).T
        pairs = range(g * pairs_per_group, (g + 1) * pairs_per_group)
        states = [state_ref[p] for p in pairs]
        y_off = _dot(c_bf, jnp.concatenate(states, axis=1).astype(BF))
        ys = []
        for j, pidx in enumerate(pairs):
            x_pair = xc[:, LANES * pidx:LANES * (pidx + 1)]
            xms = (jnp.where(lower, x_pair, 0.0).astype(BF), jnp.where(upper, x_pair, 0.0).astype(BF))
            acc_y = None
            acc_s = None
            for r in range(2):
                h = 2 * pidx + r
                acol = jnp.broadcast_to(acum_col[:, h:h + 1], (L, L))
                m_h = cb * jnp.exp(jnp.where(causal, acol - shifted_row[h:h + 1, :], NEG_BIG))
                y = _dot(m_h.astype(BF), xms[r])
                acc_y = y if acc_y is None else acc_y + y
                bw = (b_gt * wrow[h:h + 1, :]).astype(BF)
                sn = _dot(bw, xms[r])
                acc_s = sn if acc_s is None else acc_s + sn
            h0, h1 = 2 * pidx, 2 * pidx + 1
            dec = jnp.where(upper, jnp.broadcast_to(decay_col[:, h1:h1 + 1], (L, LANES)),
                            jnp.broadcast_to(decay_col[:, h0:h0 + 1], (L, LANES)))
            d0 = jnp.broadcast_to(last_col[:, h0:h0 + 1], (1, LANES))
            d1 = jnp.broadcast_to(last_col[:, h1:h1 + 1], (1, LANES))
            state_ref[pidx] = states[j] * jnp.where(upper1, d1, d0) + acc_s
            ys.append(acc_y + y_off[:, LANES * j:LANES * (j + 1)] * dec
                      + x_pair * dskip_ref[:, LANES * pidx:LANES * (pidx + 1)])
        y_g = jnp.concatenate(ys, axis=1)
        z_g = z_ref[:, group_w * g:group_w * (g + 1)].astype(F32)
        y_g = y_g * (z_g * jax.nn.sigmoid(z_g))
        y_g = _rms(y_g, nw_ref[:, group_w * g:group_w * (g + 1)])
        o_ref[:, group_w * g:group_w * (g + 1)] = y_g.astype(BF)


def _ssd_kernel(xc_ref, z_ref, dtc_ref, dtr_ref, dtbc_ref, dtbr_ref, alc_ref, alr_ref,
                dskip_ref, nw_ref, o_ref, state_ref):
    @pl.when(pl.program_id(1) == 0)
    def _():
        state_ref[...] = jnp.zeros_like(state_ref)

    for sub in range(SSD_CHUNKS):
        rows = pl.ds(CHUNK * sub, CHUNK)
        _ssd_chunk(xc_ref.at[rows], z_ref.at[rows], dtc_ref.at[rows], dtr_ref[:, CHUNK * sub:CHUNK * (sub + 1)],
                   dtbc_ref, dtbr_ref, alc_ref, alr_ref, dskip_ref, nw_ref, o_ref.at[rows], state_ref)


def _ssd(xc, z, dt_raw, dt_bias, a_log, d_skip, norm_w, bsz, seq):
    rows = CHUNK * SSD_CHUNKS
    nc = seq // rows
    n = bsz * seq
    pad = LANES - SSM_HEADS
    dt_row = dt_raw[:, :SSM_HEADS].T
    dtb_col = jnp.pad(dt_bias, (0, pad)).reshape(1, LANES)
    al_col = jnp.pad(a_log, (0, pad)).reshape(1, LANES)
    dtb_row = jnp.broadcast_to(dt_bias[:, None], (SSM_HEADS, CHUNK))
    al_row = jnp.broadcast_to(a_log[:, None], (SSM_HEADS, CHUNK))
    dskip = jnp.repeat(d_skip, D_INNER // SSM_HEADS).reshape(1, D_INNER)
    blk = lambda w: pl.BlockSpec((rows, w), lambda b, j: (b * nc + j, 0))
    return pl.pallas_call(
        _ssd_kernel,
        out_shape=jax.ShapeDtypeStruct((n, D_INNER), BF),
        grid=(bsz, nc),
        in_specs=[
            blk(XBC_DIM), blk(D_INNER), blk(LANES),
            pl.BlockSpec((SSM_HEADS, rows), lambda b, j: (0, b * nc + j)),
            _full((1, LANES)), _full((SSM_HEADS, CHUNK)), _full((1, LANES)), _full((SSM_HEADS, CHUNK)),
            _full((1, D_INNER)), _full((1, D_INNER)),
        ],
        out_specs=blk(D_INNER),
        scratch_shapes=[pltpu.VMEM((SSM_HEADS // 2, D_STATE, LANES), F32)],
        compiler_params=_params(("parallel", "arbitrary")),
        name="ssd",
    )(xc, z, dt_raw, dt_row, dtb_col, dtb_row, al_col, al_row, dskip, norm_w.reshape(1, -1))


def _outproj_kernel(x_ref, attn_ref, ssm_ref, g_ref, wa_ref, ws_ref, wo_ref, h_ref):
    a = _dot(attn_ref[...], wa_ref[...])
    s = _dot(ssm_ref[...], ws_ref[...])
    merged = g_ref[:, :D_MODEL].astype(F32) * a + g_ref[:, D_MODEL:].astype(F32) * s
    h_ref[...] = x_ref[...] + _dot(merged.astype(BF), wo_ref[...])


def _outproj(x2, attn, ssm, gates, w_attn_o, w_ssm_o, w_out):
    n = x2.shape[0]
    tm = OUT_TM
    row = lambda w: pl.BlockSpec((tm, w), lambda i: (i, 0))
    return pl.pallas_call(
        _outproj_kernel,
        out_shape=jax.ShapeDtypeStruct((n, D_MODEL), F32),
        grid=(n // tm,),
        in_specs=[row(D_MODEL), row(Q_DIM), row(D_INNER), row(2 * D_MODEL),
                  _full((Q_DIM, D_MODEL)), _full((D_INNER, D_MODEL)), _full((D_MODEL, D_MODEL))],
        out_specs=row(D_MODEL),
        compiler_params=_params(("parallel",)),
        name="outproj",
    )(x2, attn, ssm, gates, w_attn_o.astype(BF), w_ssm_o.astype(BF), w_out.astype(BF))


def _memkv_kernel(mem_ref, nw_ref, w_ref, kv_ref):
    mn = _rms(mem_ref[...], nw_ref[...]).astype(BF)
    kv_ref[...] = _dot(mn, w_ref[...]).astype(BF)


def _memkv(mem2, norm_w, w_ckv, bsz):
    return pl.pallas_call(
        _memkv_kernel,
        out_shape=jax.ShapeDtypeStruct((bsz * MEM_TOKENS, 2 * D_MODEL), BF),
        grid=(bsz,),
        in_specs=[pl.BlockSpec((MEM_TOKENS, D_MODEL), lambda b: (b, 0)), _full((1, D_MODEL)),
                  _full((D_MODEL, 2 * D_MODEL))],
        out_specs=pl.BlockSpec((MEM_TOKENS, 2 * D_MODEL), lambda b: (b, 0)),
        compiler_params=_params(("parallel",)),
        name="memkv",
    )(mem2, norm_w.reshape(1, -1), w_ckv.astype(BF))


def _cross_kernel(h_ref, kv_ref, nw_ref, wq_ref, wo_ref, o_ref):
    h = h_ref[...]
    hn = _rms(h, nw_ref[...]).astype(BF)
    q = (_dot(hn, wq_ref[...]) * (CROSS_HEAD_DIM ** -0.5)).astype(BF)
    outs = []
    for hd in range(CROSS_HEADS):
        lo, hi = CROSS_HEAD_DIM * hd, CROSS_HEAD_DIM * (hd + 1)
        s = _dot_nt(q[:, lo:hi], kv_ref[:, lo:hi])
        p = jnp.exp(s - jnp.max(s, axis=-1, keepdims=True))
        den = jnp.sum(p, axis=-1, keepdims=True)
        outs.append((_dot(p.astype(BF), kv_ref[:, D_MODEL + lo:D_MODEL + hi]) / den).astype(BF))
    o = jnp.concatenate(outs, axis=1)
    o_ref[...] = h + _dot(o, wo_ref[...])


def _cross(h, kv, norm_w, w_cq, w_co, seq):
    n = h.shape[0]
    tm = CROSS_TM
    per_b = seq // tm
    row = pl.BlockSpec((tm, D_MODEL), lambda i: (i, 0))
    return pl.pallas_call(
        _cross_kernel,
        out_shape=jax.ShapeDtypeStruct((n, D_MODEL), F32),
        grid=(n // tm,),
        in_specs=[row, pl.BlockSpec((MEM_TOKENS, 2 * D_MODEL), lambda i: (i // per_b, 0)),
                  _full((1, D_MODEL)), _full((D_MODEL, D_MODEL)), _full((D_MODEL, D_MODEL))],
        out_specs=row,
        compiler_params=_params(("parallel",)),
        name="cross",
    )(h, kv, norm_w.reshape(1, -1), w_cq.astype(BF), w_co.astype(BF))


def _router_kernel(h_ref, nw_ref, wr_ref, br_ref, t_ref, meta_ref, cnt_ref, carry_ref):
    i = pl.program_id(0)
    tm = ROUTER_TM

    @pl.when(i == 0)
    def _():
        carry_ref[...] = jnp.zeros_like(carry_ref)

    t = _rms(h_ref[...], nw_ref[...])
    for c in range(ROW_TILES):
        t_ref[pl.ds(c, tm, stride=ROW_TILES), :] = t[:, LANES * c:LANES * (c + 1)]
    lane = lax.broadcasted_iota(I32, (tm, LANES), 1)
    t_hi = t.astype(BF)
    t_lo = (t - t_hi.astype(F32)).astype(BF)
    w_hi = wr_ref[...].astype(BF)
    w_lo = (wr_ref[...] - w_hi.astype(F32)).astype(BF)
    raw = _dot(t_hi, w_hi) + (_dot(t_lo, w_hi) + _dot(t_hi, w_lo))
    logits = jnp.where(lane < N_EXPERTS, raw + br_ref[...], NEG_BIG)
    work = logits
    vals, ids, hots = [], [], []
    for _k in range(TOP_K):
        m = jnp.max(work, axis=-1, keepdims=True)
        idx = jnp.min(jnp.where(work == m, lane, LANES), axis=-1, keepdims=True)
        hot = lane == idx
        vals.append(m)
        ids.append(idx)
        hots.append(hot)
        work = jnp.where(hot, NEG_BIG * 2.0, work)
    es = [jnp.exp(v - vals[0]) for v in vals]
    den = es[0] + es[1] + es[2] + es[3]
    member = jnp.zeros((tm, LANES), F32)
    for hot in hots:
        member = member + jnp.where(hot, 1.0, 0.0)
    ii = lax.broadcasted_iota(I32, (tm, tm), 0)
    jj = lax.broadcasted_iota(I32, (tm, tm), 1)
    strict = jnp.where(ii > jj, 1.0, 0.0).astype(BF)
    carry = carry_ref[0:1, :]
    before = _dot(strict, member.astype(BF)) + carry
    meta = jnp.zeros((tm, LANES), F32)
    for k in range(TOP_K):
        rank = jnp.sum(jnp.where(hots[k], before, 0.0), axis=-1, keepdims=True)
        meta = jnp.where(lane == k, ids[k].astype(F32), meta)
        meta = jnp.where(lane == TOP_K + k, rank, meta)
        meta = jnp.where(lane == 2 * TOP_K + k, es[k] / den, meta)
    meta_ref[...] = meta
    new_carry = carry + jnp.sum(member, axis=0, keepdims=True)
    carry_ref[...] = jnp.broadcast_to(new_carry, carry_ref.shape)
    cnt_ref[...] = jnp.broadcast_to(new_carry, cnt_ref.shape)


def _router(h, norm_w, w_router, b_router):
    n = h.shape[0]
    tm = ROUTER_TM
    wr = jnp.pad(w_router, ((0, 0), (0, LANES - N_EXPERTS)))
    br = jnp.pad(b_router, (0, LANES - N_EXPERTS)).reshape(1, LANES)
    return pl.pallas_call(
        _router_kernel,
        out_shape=(jax.ShapeDtypeStruct((n * ROW_TILES, LANES), F32),
                   jax.ShapeDtypeStruct((n, LANES), F32),
                   jax.ShapeDtypeStruct((SUBLANES, LANES), F32)),
        grid=(n // tm,),
        in_specs=[pl.BlockSpec((tm, D_MODEL), lambda i: (i, 0)), _full((1, D_MODEL)),
                  _full((D_MODEL, LANES)), _full((1, LANES))],
        out_specs=[pl.BlockSpec((tm * ROW_TILES, LANES), lambda i: (i, 0)),
                   pl.BlockSpec((tm, LANES), lambda i: (i, 0)),
                   _full((SUBLANES, LANES))],
        scratch_shapes=[pltpu.VMEM((SUBLANES, LANES), F32)],
        compiler_params=_params(("arbitrary",)),
        name="router",
    )(h, norm_w.reshape(1, -1), wr, br)


def _row(ref, r):
    return ref.at[pl.ds(pl.multiple_of(r * ROW_TILES, ROW_TILES), ROW_TILES)]


def _dispatch_kernel(dest_ref, t_ref, xs_hbm, sem):
    tt = DISPATCH_TT

    def copy(j, k):
        return pltpu.make_async_copy(_row(t_ref, j), _row(xs_hbm, dest_ref[j * TOP_K + k]), sem)

    def issue(j, carry):
        for k in range(TOP_K):
            copy(j, k).start()
        return carry

    def drain(j, carry):
        for k in range(TOP_K):
            copy(j, k).wait()
        return carry

    lax.fori_loop(0, tt, issue, 0)
    lax.fori_loop(0, tt, drain, 0)


def _dispatch(t_rows, dest):
    n = t_rows.shape[0] // ROW_TILES
    tt = DISPATCH_TT
    return pl.pallas_call(
        _dispatch_kernel,
        out_shape=jax.ShapeDtypeStruct((n * TOP_K * ROW_TILES, LANES), F32),
        grid=(n // tt,),
        in_specs=[pl.BlockSpec((tt * TOP_K,), lambda i: (i,), memory_space=pltpu.SMEM),
                  pl.BlockSpec((tt * ROW_TILES, LANES), lambda i: (i, 0))],
        out_specs=pl.BlockSpec(memory_space=pl.ANY),
        scratch_shapes=[pltpu.SemaphoreType.DMA],
        compiler_params=_params(("arbitrary",)),
        name="dispatch",
    )(dest, t_rows)


def _expert_kernel(tile_ref, exp_ref, lo_ref, hi_ref, first_ref, newexp_ref, slot_ref, next_ref,
                   xs_ref, wgu_hbm, bgu_ref, wdn_hbm, bdn_ref, ys_ref, wgu_bf, wdn_bf, wgu_f32, wdn_f32, sems):
    w = pl.program_id(0)
    tm = EXPERT_TM
    lo = lo_ref[w]
    hi = hi_ref[w]

    def weight_copies(e, slot):
        return (pltpu.make_async_copy(wgu_hbm.at[e], wgu_f32.at[slot], sems.at[0, slot]),
                pltpu.make_async_copy(wdn_hbm.at[e], wdn_f32.at[slot], sems.at[1, slot]))

    @pl.when(w == 0)
    def _():
        for cp in weight_copies(exp_ref[0], 0):
            cp.start()

    @pl.when(newexp_ref[w] == 1)
    def _():
        slot = slot_ref[w]
        for cp in weight_copies(exp_ref[w], slot):
            cp.wait()
        wgu_bf[...] = wgu_f32[slot].astype(BF)
        wdn_bf[...] = wdn_f32[slot].astype(BF)

        @pl.when(next_ref[w] >= 0)
        def _():
            for cp in weight_copies(next_ref[w], 1 - slot):
                cp.start()

    @pl.when(hi > lo)
    def _():
        x = jnp.concatenate([xs_ref[pl.ds(c, tm, stride=ROW_TILES), :] for c in range(ROW_TILES)],
                            axis=1).astype(BF)
        gu = _dot(x, wgu_bf[...]) + bgu_ref[0]
        g = jnp.minimum(gu[:, :D_EXPERT], SWIGLU_LIMIT)
        up = jnp.clip(gu[:, D_EXPERT:], -SWIGLU_LIMIT, SWIGLU_LIMIT)
        act = (up + 1.0) * (g * jax.nn.sigmoid(SWIGLU_ALPHA * g))
        y = _dot(act.astype(BF), wdn_bf[...]) + bdn_ref[0]
        rows = tile_ref[w] * tm + lax.broadcasted_iota(I32, (tm, LANES), 0)
        mine = (rows >= lo) & (rows < hi)
        is_first = first_ref[w] == 1

        @pl.when(is_first)
        def _():
            for c in range(ROW_TILES):
                ys_ref[pl.ds(c, tm, stride=ROW_TILES), :] = jnp.where(mine, y[:, LANES * c:LANES * (c + 1)], 0.0)

        @pl.when(jnp.logical_not(is_first))
        def _():
            for c in range(ROW_TILES):
                old = ys_ref[pl.ds(c, tm, stride=ROW_TILES), :]
                ys_ref[pl.ds(c, tm, stride=ROW_TILES), :] = jnp.where(mine, y[:, LANES * c:LANES * (c + 1)], old)


def _experts(xs, items, w_gu, b_gu, w_dn, b_dn):
    n_items = items[0].shape[0]
    tm = EXPERT_TM
    grid_spec = pltpu.PrefetchScalarGridSpec(
        num_scalar_prefetch=len(items),
        grid=(n_items,),
        in_specs=[
            pl.BlockSpec((tm * ROW_TILES, LANES), lambda w, t, e, *_: (t[w], 0)),
            pl.BlockSpec(memory_space=pl.ANY),
            pl.BlockSpec((1, 1, 2 * D_EXPERT), lambda w, t, e, *_: (e[w], 0, 0)),
            pl.BlockSpec(memory_space=pl.ANY),
            pl.BlockSpec((1, 1, D_MODEL), lambda w, t, e, *_: (e[w], 0, 0)),
        ],
        out_specs=pl.BlockSpec((tm * ROW_TILES, LANES), lambda w, t, e, *_: (t[w], 0)),
        scratch_shapes=[pltpu.VMEM((D_MODEL, 2 * D_EXPERT), BF), pltpu.VMEM((D_EXPERT, D_MODEL), BF),
                        pltpu.VMEM((2, D_MODEL, 2 * D_EXPERT), F32), pltpu.VMEM((2, D_EXPERT, D_MODEL), F32),
                        pltpu.SemaphoreType.DMA((2, 2))],
    )
    return pl.pallas_call(
        _expert_kernel,
        out_shape=jax.ShapeDtypeStruct(xs.shape, F32),
        grid_spec=grid_spec,
        compiler_params=_params(("arbitrary",)),
        name="experts",
    )(*items, xs, w_gu, b_gu.reshape(N_EXPERTS, 1, -1), w_dn, b_dn.reshape(N_EXPERTS, 1, -1))


def _expert_items(counts, n_rows):
    tm = EXPERT_TM
    n_tiles = n_rows // tm
    n_items = n_tiles + N_EXPERTS - 1
    end = jnp.cumsum(counts)
    off = end - counts
    first_tile = off // tm
    last_tile = jnp.maximum(end - 1, 0) // tm
    per = jnp.where(counts > 0, last_tile - first_tile + 1, 0)
    cum = jnp.cumsum(per)
    start = cum - per
    total = cum[-1]
    w = jnp.arange(n_items, dtype=I32)
    wc = jnp.minimum(w, total - 1)
    e_of = jnp.minimum(jnp.sum((cum[None, :] <= wc[:, None]).astype(I32), axis=1), N_EXPERTS - 1)
    sel = e_of[:, None] == jnp.arange(N_EXPERTS, dtype=I32)
    pick = lambda v: jnp.sum(jnp.where(sel, v[None, :], 0), axis=1)
    t_of = (pick(first_tile) + (wc - pick(start))).astype(I32)
    valid = w < total
    lo = jnp.where(valid, jnp.maximum(pick(off), t_of * tm), 0).astype(I32)
    hi = jnp.where(valid, jnp.minimum(pick(end), (t_of + 1) * tm), 0).astype(I32)
    prev_t = jnp.concatenate([jnp.full((1,), -1, I32), t_of[:-1]])
    first = (t_of != prev_t).astype(I32)
    prev_e = jnp.concatenate([jnp.full((1,), -1, I32), e_of[:-1]])
    newexp = (e_of != prev_e).astype(I32)
    slot = (jnp.cumsum(newexp) - 1) % 2
    later_change = (w[None, :] > w[:, None]) & (newexp[None, :] == 1)
    nxt = jnp.min(jnp.where(later_change, w[None, :], n_items), axis=1)
    next_e = jnp.sum(jnp.where(nxt[:, None] == w[None, :], e_of[None, :], 0), axis=1)
    next_e = jnp.where(nxt < n_items, next_e, -1).astype(I32)
    return t_of, e_of, lo, hi, first, newexp, slot.astype(I32), next_e


def _combine_kernel(dest_ref, meta_ref, h_ref, fw_ref, ys_hbm, o_ref, buf_ref, sems):
    tt = COMBINE_TT
    tp = tt // COMBINE_PHASES

    def copy(j, k, p):
        return pltpu.make_async_copy(_row(ys_hbm, dest_ref[j * TOP_K + k]), _row(buf_ref, k * tt + j), sems.at[p])

    for p in range(COMBINE_PHASES):
        def issue(j, carry, p=p):
            for k in range(TOP_K):
                copy(j, k, p).start()
            return carry
        lax.fori_loop(p * tp, (p + 1) * tp, issue, 0)

    for p in range(COMBINE_PHASES):
        def drain(j, carry, p=p):
            for k in range(TOP_K):
                copy(j, k, p).wait()
            return carry
        lax.fori_loop(p * tp, (p + 1) * tp, drain, 0)
        rows = pl.ds(p * tp, tp)
        meta = meta_ref[rows, :]
        wts = [jnp.broadcast_to(meta[:, 2 * TOP_K + k:2 * TOP_K + k + 1], (tp, LANES)) for k in range(TOP_K)]
        cols = []
        for c in range(ROW_TILES):
            acc = h_ref[rows, LANES * c:LANES * (c + 1)]
            for k in range(TOP_K):
                acc = acc + wts[k] * buf_ref[pl.ds((k * tt + p * tp) * ROW_TILES + c, tp, stride=ROW_TILES), :]
            cols.append(acc)
        o_ref[rows, :] = _rms(jnp.concatenate(cols, axis=1), fw_ref[...])


def _combine(dest, meta, h, final_w, ys):
    n = h.shape[0]
    tt = COMBINE_TT
    return pl.pallas_call(
        _combine_kernel,
        out_shape=jax.ShapeDtypeStruct((n, D_MODEL), F32),
        grid=(n // tt,),
        in_specs=[pl.BlockSpec((tt * TOP_K,), lambda i: (i,), memory_space=pltpu.SMEM),
                  pl.BlockSpec((tt, LANES), lambda i: (i, 0)),
                  pl.BlockSpec((tt, D_MODEL), lambda i: (i, 0)),
                  _full((1, D_MODEL)),
                  pl.BlockSpec(memory_space=pl.ANY)],
        out_specs=pl.BlockSpec((tt, D_MODEL), lambda i: (i, 0)),
        scratch_shapes=[pltpu.VMEM((TOP_K * tt * ROW_TILES, LANES), F32),
                        pltpu.SemaphoreType.DMA((COMBINE_PHASES,))],
        compiler_params=_params(("arbitrary",)),
        name="combine",
    )(dest, meta, h, final_w.reshape(1, -1), ys)


def _moe(h, norm_w, w_router, b_router, w_gu, b_gu, w_dn, b_dn, final_w):
    n = h.shape[0]
    t_rows, meta, cnt = _router(h, norm_w, w_router, b_router)
    eid = meta[:, :TOP_K].astype(I32)
    rank = meta[:, TOP_K:2 * TOP_K].astype(I32)
    counts = cnt[0, :N_EXPERTS].astype(I32)
    off = jnp.cumsum(counts) - counts
    hot = eid[..., None] == jnp.arange(N_EXPERTS, dtype=I32)
    dest = (jnp.sum(jnp.where(hot, off, 0), axis=-1) + rank).reshape(-1)
    xs = _dispatch(t_rows, dest)
    ys = _experts(xs, _expert_items(counts, n * TOP_K), w_gu, b_gu, w_dn, b_dn)
    return _combine(dest, meta, h, final_w, ys)


def kernel(x, mem, norm_mix_w, w_in, w_gate, b_gate, attn_sinks, conv_w, conv_b, dt_bias, a_log, d_skip,
           ssm_norm_w, w_attn_o, w_ssm_o, w_out, norm_cross_w, norm_mem_w, w_cq, w_ckv, w_co,
           norm_ffn_w, w_router, b_router, w_gu, b_gu, w_dn, b_dn, final_norm_w):
    bsz, seq, _ = x.shape
    assert norm_mix_w.shape[0] == 1, "single-layer block"
    x2 = x.reshape(bsz * seq, D_MODEL)
    q, kv, z, xc, dt_raw, gates = _proj(x2, norm_mix_w[0], w_in[0], w_gate[0], b_gate[0], conv_w[0], conv_b[0], seq)
    attn = _swa(q, kv, attn_sinks[0], bsz, seq)
    ssm = _ssd(xc, z, dt_raw, dt_bias[0], a_log[0], d_skip[0], ssm_norm_w[0], bsz, seq)
    h = _outproj(x2, attn, ssm, gates, w_attn_o[0], w_ssm_o[0], w_out[0])
    mkv = _memkv(mem.reshape(bsz * MEM_TOKENS, D_MODEL), norm_mem_w[0], w_ckv[0], bsz)
    h = _cross(h, mkv, norm_cross_w[0], w_cq[0], w_co[0], seq)
    out = _moe(h, norm_ffn_w[0], w_router[0], b_router[0], w_gu[0], b_gu[0], w_dn[0], b_dn[0], final_norm_w)
    return out.reshape(bsz, seq, D_MODEL)
```

```python
import functools

import jax
import jax.numpy as jnp
from jax import lax
from jax.experimental import pallas as pl
from jax.experimental.pallas import tpu as pltpu

BF = jnp.bfloat16
F32 = jnp.float32
I32 = jnp.int32

D_MODEL = 1024
RMS_EPS = 1e-5
ATTN_HEADS = 16
ATTN_KV_HEADS = 2
HEAD_DIM = 64
WINDOW = 128
Q_DIM = ATTN_HEADS * HEAD_DIM
KV_DIM = ATTN_KV_HEADS * HEAD_DIM
D_INNER = 2 * D_MODEL
SSM_HEADS = 32
SSM_GROUPS = 4
D_STATE = 128
CONV_WIDTH = 4
CHUNK = 128
XBC_DIM = D_INNER + 2 * SSM_GROUPS * D_STATE
MEM_TOKENS = 256
CROSS_HEADS = 4
CROSS_HEAD_DIM = D_MODEL // CROSS_HEADS
N_EXPERTS = 32
TOP_K = 4
D_EXPERT = D_MODEL
SWIGLU_LIMIT = 7.0
SWIGLU_ALPHA = 1.702

LANES = 128
SUBLANES = 8
ROW_TILES = D_MODEL // LANES
NEG_BIG = -1e30
VMEM_LIMIT = 56 * 1024 * 1024

PROJ_TM = 256
DMA_PRIORITIES = 2
OUT_TM = 512
CROSS_TM = 512
ROUTER_TM = 512
DISPATCH_TT = 512
EXPERT_TM = 256
COMBINE_TT = 512
COMBINE_PHASES = 2


def _rms(x, w):
    return x * lax.rsqrt(jnp.mean(x * x, axis=-1, keepdims=True) + RMS_EPS) * w


def _dot(a, b):
    return jnp.dot(a, b, preferred_element_type=F32)


def _dot_nt(a, b):
    return lax.dot_general(a, b, (((1,), (1,)), ((), ())), preferred_element_type=F32)


def _dot_f32(a, b):
    return jnp.dot(a, b, preferred_element_type=F32, precision=lax.Precision.HIGHEST)


def _params(sem):
    return pltpu.CompilerParams(dimension_semantics=sem, vmem_limit_bytes=VMEM_LIMIT)


def _full(shape):
    nd = len(shape)
    return pl.BlockSpec(shape, lambda *_: (0,) * nd)


def _proj_kernel(x_ref, nw_ref, wq_ref, wkv_ref, wz_ref, wxbc_ref, wdt_ref, wg_ref, bg_ref,
                 q_ref, kv_ref, z_ref, xbc_ref, dt_ref, g_ref):
    u = _rms(x_ref[...], nw_ref[...]).astype(BF)
    q_ref[...] = _dot(u, wq_ref[...]).astype(BF)
    kv_ref[...] = _dot(u, wkv_ref[...]).astype(BF)
    z_ref[...] = _dot(u, wz_ref[...]).astype(BF)
    xbc_ref[...] = _dot(u, wxbc_ref[...]).astype(BF)
    dt_ref[...] = _dot(u, wdt_ref[...])
    g_ref[...] = jax.nn.sigmoid(_dot(u, wg_ref[...]) + bg_ref[...]).astype(BF)


def _proj(x2, norm_w, w_in, w_gate, b_gate):
    n = x2.shape[0]
    tm = PROJ_TM
    s0, s1, s2, s3, s4 = Q_DIM, Q_DIM + KV_DIM, Q_DIM + 2 * KV_DIM, Q_DIM + 2 * KV_DIM + D_INNER, \
        Q_DIM + 2 * KV_DIM + D_INNER + XBC_DIM
    wq = (w_in[:, :s0] * (HEAD_DIM ** -0.5)).astype(BF)
    wk, wv = w_in[:, s0:s1], w_in[:, s1:s2]
    dup = lambda w: jnp.concatenate([w[:, :HEAD_DIM], w[:, :HEAD_DIM], w[:, HEAD_DIM:], w[:, HEAD_DIM:]], axis=1)
    wkv = jnp.concatenate([dup(wk), dup(wv)], axis=1).astype(BF)
    wz = w_in[:, s2:s3].astype(BF)
    wxbc = w_in[:, s3:s4].astype(BF)
    wdt = jnp.pad(w_in[:, s4:], ((0, 0), (0, LANES - SSM_HEADS))).astype(BF)
    wg = w_gate.astype(BF)
    outs = (
        jax.ShapeDtypeStruct((n, Q_DIM), BF),
        jax.ShapeDtypeStruct((n, 4 * LANES), BF),
        jax.ShapeDtypeStruct((n, D_INNER), BF),
        jax.ShapeDtypeStruct((n, XBC_DIM), BF),
        jax.ShapeDtypeStruct((n, LANES), F32),
        jax.ShapeDtypeStruct((n, 2 * D_MODEL), BF),
    )
    row = lambda w: pl.BlockSpec((tm, w), lambda i: (i, 0))
    return pl.pallas_call(
        _proj_kernel,
        out_shape=outs,
        grid=(n // tm,),
        in_specs=[row(D_MODEL), _full((1, D_MODEL)), _full(wq.shape), _full(wkv.shape), _full(wz.shape),
                  _full(wxbc.shape), _full(wdt.shape), _full(wg.shape), _full((1, 2 * D_MODEL))],
        out_specs=[row(Q_DIM), row(4 * LANES), row(D_INNER), row(XBC_DIM), row(LANES), row(2 * D_MODEL)],
        compiler_params=_params(("parallel",)),
        name="proj",
    )(x2, norm_w.reshape(1, -1), wq, wkv, wz, wxbc, wdt, wg, b_gate.reshape(1, -1))


PAIRS_PER_KV = ATTN_HEADS // ATTN_KV_HEADS // 2


SWA_BLOCKS = 2


def _swa_block(sink_ref, q_ref, kv_prev, kv_cur, bias, o_ref):
    w = WINDOW
    upper = lax.broadcasted_iota(I32, (2 * w, LANES), 1) >= HEAD_DIM
    zero = jnp.zeros((2 * w, LANES), BF)
    ones = jnp.ones((2 * w, LANES), BF)
    chains = [(kvh, r) for kvh in range(ATTN_KV_HEADS) for r in range(2)]
    half = lambda t, r: jnp.where(upper, t, zero) if r else jnp.where(upper, zero, t)
    scores, values, sinks = [], [], []
    for kvh, r in chains:
        ks = slice(LANES * kvh, LANES * (kvh + 1))
        vs = slice(LANES * (ATTN_KV_HEADS + kvh), LANES * (ATTN_KV_HEADS + kvh + 1))
        k2 = jnp.concatenate([kv_prev[:, ks], kv_cur[:, ks]], axis=0)
        v2 = jnp.concatenate([kv_prev[:, vs], kv_cur[:, vs]], axis=0)
        q4 = jnp.concatenate([q_ref[:, LANES * (kvh * PAIRS_PER_KV + p):LANES * (kvh * PAIRS_PER_KV + p + 1)]
                              for p in range(PAIRS_PER_KV)], axis=0)
        scores.append(_dot_nt(q4, half(k2, r)) + bias(kvh, r))
        values.append(jnp.concatenate([half(v2, r), ones], axis=1))
        sinks.append(jnp.concatenate(
            [jnp.full((w, 1), sink_ref[2 * (kvh * PAIRS_PER_KV + p) + r], F32) for p in range(PAIRS_PER_KV)],
            axis=0))
    maxes = [jnp.maximum(jnp.max(s, axis=-1, keepdims=True), sk) for s, sk in zip(scores, sinks)]
    probs = [jnp.exp(s - m).astype(BF) for s, m in zip(scores, maxes)]
    nds = [_dot(p, v) for p, v in zip(probs, values)]
    outs = [nd[:, :LANES] / (nd[:, LANES:] + jnp.exp(sk - m)) for nd, sk, m in zip(nds, sinks, maxes)]
    for kvh in range(ATTN_KV_HEADS):
        o4 = outs[2 * kvh] + outs[2 * kvh + 1]
        for p in range(PAIRS_PER_KV):
            pidx = kvh * PAIRS_PER_KV + p
            o_ref[:, LANES * pidx:LANES * (pidx + 1)] = o4[w * p:w * (p + 1)].astype(BF)


def _swa_kernel(sink_ref, q_ref, kvc_ref, kvp_ref, bias_ref, o_ref):
    has_prev = jnp.minimum(pl.program_id(1), 1)
    for sub in range(SWA_BLOCKS):
        rows = pl.ds(WINDOW * sub, WINDOW)
        kv_prev = kvp_ref if sub == 0 else kvc_ref.at[pl.ds(WINDOW * (sub - 1), WINDOW)]
        variant = has_prev if sub == 0 else 1
        _swa_block(sink_ref, q_ref.at[rows], kv_prev, kvc_ref.at[rows],
                   lambda kvh, r, v=variant: bias_ref[v, kvh, r], o_ref.at[rows])


def _swa_bias():
    qi = jnp.arange(WINDOW)[:, None]
    kj = jnp.arange(2 * WINDOW)[None, :]
    dist = qi + WINDOW - kj
    in_window = (dist >= 0) & (dist < WINDOW)
    heads = jnp.arange(ATTN_HEADS).reshape(ATTN_KV_HEADS, PAIRS_PER_KV, 2).transpose(0, 2, 1)
    slopes = jnp.exp2(-8.0 * (heads + 1).astype(F32) / ATTN_HEADS)
    bias = -slopes[..., None, None] * dist.astype(F32)
    tabs = []
    for has_prev in (False, True):
        mask = in_window & (has_prev | (kj >= WINDOW))
        tabs.append(jnp.where(mask, bias, NEG_BIG).reshape(ATTN_KV_HEADS, 2, PAIRS_PER_KV * WINDOW, 2 * WINDOW))
    return jnp.stack(tabs)


def _swa(q, kv, sinks, bsz, seq):
    nb = seq // (WINDOW * SWA_BLOCKS)
    n = bsz * seq
    rows = WINDOW * SWA_BLOCKS
    return pl.pallas_call(
        _swa_kernel,
        out_shape=jax.ShapeDtypeStruct((n, Q_DIM), BF),
        grid=(bsz, nb),
        in_specs=[
            pl.BlockSpec(memory_space=pltpu.SMEM),
            pl.BlockSpec((rows, Q_DIM), lambda b, j: (b * nb + j, 0)),
            pl.BlockSpec((rows, 4 * LANES), lambda b, j: (b * nb + j, 0)),
            pl.BlockSpec((WINDOW, 4 * LANES), lambda b, j: (jnp.maximum((b * nb + j) * SWA_BLOCKS - 1, 0), 0)),
            _full((2, ATTN_KV_HEADS, 2, PAIRS_PER_KV * WINDOW, 2 * WINDOW)),
        ],
        out_specs=pl.BlockSpec((rows, Q_DIM), lambda b, j: (b * nb + j, 0)),
        compiler_params=_params(("parallel", "parallel")),
        name="swa",
    )(sinks.astype(F32), q, kv, kv, _swa_bias())


def _softplus(x):
    return jnp.maximum(x, 0.0) + jnp.log1p(jnp.exp(-jnp.abs(x)))


SSD_CHUNKS = 2


def _ssd_chunk(xbc_ref, z_ref, dtc_ref, dt_row_raw, cw_ref, cb_ref, dtbc_ref, dtbr_ref, alc_ref, alr_ref,
               dskip_ref, nw_ref, o_ref, state_ref, tail_ref):
    L = CHUNK
    x_raw = xbc_ref[...].astype(F32)
    prev8 = tail_ref[...]
    row8 = lax.broadcasted_iota(I32, (SUBLANES, XBC_DIM), 0)
    acc = x_raw * cw_ref[CONV_WIDTH - 1:CONV_WIDTH, :] + cb_ref[...]
    for s in range(1, CONV_WIDTH):
        sh = pltpu.roll(x_raw, s, axis=0)
        top = jnp.where(row8 < s, pltpu.roll(prev8, s, axis=0), sh[:SUBLANES])
        sh = jnp.concatenate([top, sh[SUBLANES:]], axis=0)
        acc = acc + sh * cw_ref[CONV_WIDTH - 1 - s:CONV_WIDTH - s, :]
    tail_ref[...] = x_raw[L - SUBLANES:, :]
    xc = acc * jax.nn.sigmoid(acc)

    dt_col = _softplus(dtc_ref[...] + dtbc_ref[...])
    dt_row = _softplus(dt_row_raw + dtbr_ref[...])
    a_col = dt_col * (-jnp.exp(alc_ref[...]))
    a_row = dt_row * (-jnp.exp(alr_ref[...]))
    ii = lax.broadcasted_iota(I32, (L, L), 0)
    jj = lax.broadcasted_iota(I32, (L, L), 1)
    causal = ii >= jj
    tri_l = jnp.where(causal, 1.0, 0.0).astype(F32)
    tri_u = jnp.where(ii <= jj, 1.0, 0.0).astype(F32)
    acum_col = _dot_f32(tri_l, a_col)
    acum_row = _dot_f32(a_row, tri_u)
    decay_col = jnp.exp(acum_col)
    last_col = decay_col[L - 1:L, :]
    last_row = acum_row[:, L - 1:L]
    wrow = jnp.exp(last_row - acum_row) * dt_row
    shifted_row = acum_row - jnp.log(dt_row)

    upper = lax.broadcasted_iota(I32, (L, LANES), 1) >= 64
    upper1 = upper[:1]
    lower = jnp.logical_not(upper)
    pairs_per_group = SSM_HEADS // SSM_GROUPS // 2
    group_w = D_INNER // SSM_GROUPS
    for g in range(SSM_GROUPS):
        b_g = xc[:, D_INNER + D_STATE * g:D_INNER + D_STATE * (g + 1)]
        c_g = xc[:, D_INNER + D_STATE * (SSM_GROUPS + g):D_INNER + D_STATE * (SSM_GROUPS + g + 1)]
        c_bf = c_g.astype(BF)
        cb = _dot_nt(c_bf, b_g.astype(BF))
        b_gt = b_g.astype(F32).T
        pairs = range(g * pairs_per_group, (g + 1) * pairs_per_group)
        states = [state_ref[p] for p in pairs]
        y_off = _dot(c_bf, jnp.concatenate(states, axis=1).astype(BF))
        ys = []
        for j, pidx in enumerate(pairs):
            x_pair = xc[:, LANES * pidx:LANES * (pidx + 1)]
            xms = (jnp.where(lower, x_pair, 0.0).astype(BF), jnp.where(upper, x_pair, 0.0).astype(BF))
            acc_y = None
            acc_s = None
            for r in range(2):
                h = 2 * pidx + r
                acol = jnp.broadcast_to(acum_col[:, h:h + 1], (L, L))
                m_h = cb * jnp.exp(jnp.where(causal, acol - shifted_row[h:h + 1, :], NEG_BIG))
                y = _dot(m_h.astype(BF), xms[r])
                acc_y = y if acc_y is None else acc_y + y
                bw = (b_gt * wrow[h:h + 1, :]).astype(BF)
                sn = _dot(bw, xms[r])
                acc_s = sn if acc_s is None else acc_s + sn
            h0, h1 = 2 * pidx, 2 * pidx + 1
            dec = jnp.where(upper, jnp.broadcast_to(decay_col[:, h1:h1 + 1], (L, LANES)),
                            jnp.broadcast_to(decay_col[:, h0:h0 + 1], (L, LANES)))
            d0 = jnp.broadcast_to(last_col[:, h0:h0 + 1], (1, LANES))
            d1 = jnp.broadcast_to(last_col[:, h1:h1 + 1], (1, LANES))
            state_ref[pidx] = states[j] * jnp.where(upper1, d1, d0) + acc_s
            ys.append(acc_y + y_off[:, LANES * j:LANES * (j + 1)] * dec
                      + x_pair * dskip_ref[:, LANES * pidx:LANES * (pidx + 1)])
        y_g = jnp.concatenate(ys, axis=1)
        z_g = z_ref[:, group_w * g:group_w * (g + 1)].astype(F32)
        y_g = y_g * (z_g * jax.nn.sigmoid(z_g))
        y_g = _rms(y_g, nw_ref[:, group_w * g:group_w * (g + 1)])
        o_ref[:, group_w * g:group_w * (g + 1)] = y_g.astype(BF)


def _ssd_kernel(xbc_ref, z_ref, dtc_ref, dtr_ref, cw_ref, cb_ref, dtbc_ref, dtbr_ref, alc_ref, alr_ref,
                dskip_ref, nw_ref, o_ref, state_ref, tail_ref):
    @pl.when(pl.program_id(1) == 0)
    def _():
        state_ref[...] = jnp.zeros_like(state_ref)
        tail_ref[...] = jnp.zeros_like(tail_ref)

    for sub in range(SSD_CHUNKS):
        rows = pl.ds(CHUNK * sub, CHUNK)
        _ssd_chunk(xbc_ref.at[rows], z_ref.at[rows], dtc_ref.at[rows], dtr_ref[:, CHUNK * sub:CHUNK * (sub + 1)],
                   cw_ref, cb_ref, dtbc_ref, dtbr_ref, alc_ref, alr_ref, dskip_ref, nw_ref, o_ref.at[rows],
                   state_ref, tail_ref)


def _ssd(xbc, z, dt_raw, conv_w, conv_b, dt_bias, a_log, d_skip, norm_w, bsz, seq):
    rows = CHUNK * SSD_CHUNKS
    nc = seq // rows
    n = bsz * seq
    pad = LANES - SSM_HEADS
    dt_row = dt_raw[:, :SSM_HEADS].T
    dtb_col = jnp.pad(dt_bias, (0, pad)).reshape(1, LANES)
    al_col = jnp.pad(a_log, (0, pad)).reshape(1, LANES)
    dtb_row = jnp.broadcast_to(dt_bias[:, None], (SSM_HEADS, CHUNK))
    al_row = jnp.broadcast_to(a_log[:, None], (SSM_HEADS, CHUNK))
    dskip = jnp.repeat(d_skip, D_INNER // SSM_HEADS).reshape(1, D_INNER)
    blk = lambda w: pl.BlockSpec((rows, w), lambda b, j: (b * nc + j, 0))
    return pl.pallas_call(
        _ssd_kernel,
        out_shape=jax.ShapeDtypeStruct((n, D_INNER), BF),
        grid=(bsz, nc),
        in_specs=[
            blk(XBC_DIM), blk(D_INNER), blk(LANES),
            pl.BlockSpec((SSM_HEADS, rows), lambda b, j: (0, b * nc + j)),
            _full((CONV_WIDTH, XBC_DIM)), _full((1, XBC_DIM)),
            _full((1, LANES)), _full((SSM_HEADS, CHUNK)), _full((1, LANES)), _full((SSM_HEADS, CHUNK)),
            _full((1, D_INNER)), _full((1, D_INNER)),
        ],
        out_specs=blk(D_INNER),
        scratch_shapes=[pltpu.VMEM((SSM_HEADS // 2, D_STATE, LANES), F32),
                        pltpu.VMEM((SUBLANES, XBC_DIM), F32)],
        compiler_params=_params(("parallel", "arbitrary")),
        name="ssd",
    )(xbc, z, dt_raw, dt_row, conv_w, conv_b.reshape(1, -1), dtb_col, dtb_row, al_col, al_row,
      dskip, norm_w.reshape(1, -1))


def _outproj_kernel(x_ref, attn_ref, ssm_ref, g_ref, wa_ref, ws_ref, wo_ref, h_ref):
    a = _dot(attn_ref[...], wa_ref[...])
    s = _dot(ssm_ref[...], ws_ref[...])
    merged = g_ref[:, :D_MODEL].astype(F32) * a + g_ref[:, D_MODEL:].astype(F32) * s
    h_ref[...] = x_ref[...] + _dot(merged.astype(BF), wo_ref[...])


def _outproj(x2, attn, ssm, gates, w_attn_o, w_ssm_o, w_out):
    n = x2.shape[0]
    tm = OUT_TM
    row = lambda w: pl.BlockSpec((tm, w), lambda i: (i, 0))
    return pl.pallas_call(
        _outproj_kernel,
        out_shape=jax.ShapeDtypeStruct((n, D_MODEL), F32),
        grid=(n // tm,),
        in_specs=[row(D_MODEL), row(Q_DIM), row(D_INNER), row(2 * D_MODEL),
                  _full((Q_DIM, D_MODEL)), _full((D_INNER, D_MODEL)), _full((D_MODEL, D_MODEL))],
        out_specs=row(D_MODEL),
        compiler_params=_params(("parallel",)),
        name="outproj",
    )(x2, attn, ssm, gates, w_attn_o.astype(BF), w_ssm_o.astype(BF), w_out.astype(BF))


def _memkv_kernel(mem_ref, nw_ref, w_ref, kv_ref):
    mn = _rms(mem_ref[...], nw_ref[...]).astype(BF)
    kv_ref[...] = _dot(mn, w_ref[...]).astype(BF)


def _memkv(mem2, norm_w, w_ckv, bsz):
    return pl.pallas_call(
        _memkv_kernel,
        out_shape=jax.ShapeDtypeStruct((bsz * MEM_TOKENS, 2 * D_MODEL), BF),
        grid=(bsz,),
        in_specs=[pl.BlockSpec((MEM_TOKENS, D_MODEL), lambda b: (b, 0)), _full((1, D_MODEL)),
                  _full((D_MODEL, 2 * D_MODEL))],
        out_specs=pl.BlockSpec((MEM_TOKENS, 2 * D_MODEL), lambda b: (b, 0)),
        compiler_params=_params(("parallel",)),
        name="memkv",
    )(mem2, norm_w.reshape(1, -1), w_ckv.astype(BF))


def _cross_kernel(h_ref, kv_ref, nw_ref, wq_ref, wo_ref, o_ref):
    h = h_ref[...]
    hn = _rms(h, nw_ref[...]).astype(BF)
    q = (_dot(hn, wq_ref[...]) * (CROSS_HEAD_DIM ** -0.5)).astype(BF)
    outs = []
    for hd in range(CROSS_HEADS):
        lo, hi = CROSS_HEAD_DIM * hd, CROSS_HEAD_DIM * (hd + 1)
        s = _dot_nt(q[:, lo:hi], kv_ref[:, lo:hi])
        p = jnp.exp(s - jnp.max(s, axis=-1, keepdims=True))
        den = jnp.sum(p, axis=-1, keepdims=True)
        outs.append((_dot(p.astype(BF), kv_ref[:, D_MODEL + lo:D_MODEL + hi]) / den).astype(BF))
    o = jnp.concatenate(outs, axis=1)
    o_ref[...] = h + _dot(o, wo_ref[...])


def _cross(h, kv, norm_w, w_cq, w_co, seq):
    n = h.shape[0]
    tm = CROSS_TM
    per_b = seq // tm
    row = pl.BlockSpec((tm, D_MODEL), lambda i: (i, 0))
    return pl.pallas_call(
        _cross_kernel,
        out_shape=jax.ShapeDtypeStruct((n, D_MODEL), F32),
        grid=(n // tm,),
        in_specs=[row, pl.BlockSpec((MEM_TOKENS, 2 * D_MODEL), lambda i: (i // per_b, 0)),
                  _full((1, D_MODEL)), _full((D_MODEL, D_MODEL)), _full((D_MODEL, D_MODEL))],
        out_specs=row,
        compiler_params=_params(("parallel",)),
        name="cross",
    )(h, kv, norm_w.reshape(1, -1), w_cq.astype(BF), w_co.astype(BF))


def _router_kernel(h_ref, nw_ref, wr_ref, br_ref, t_ref, meta_ref, cnt_ref, carry_ref):
    i = pl.program_id(0)
    tm = ROUTER_TM

    @pl.when(i == 0)
    def _():
        carry_ref[...] = jnp.zeros_like(carry_ref)

    t = _rms(h_ref[...], nw_ref[...])
    for c in range(ROW_TILES):
        t_ref[pl.ds(c, tm, stride=ROW_TILES), :] = t[:, LANES * c:LANES * (c + 1)]
    lane = lax.broadcasted_iota(I32, (tm, LANES), 1)
    t_hi = t.astype(BF)
    t_lo = (t - t_hi.astype(F32)).astype(BF)
    w_hi = wr_ref[...].astype(BF)
    w_lo = (wr_ref[...] - w_hi.astype(F32)).astype(BF)
    raw = _dot(t_hi, w_hi) + (_dot(t_lo, w_hi) + _dot(t_hi, w_lo))
    logits = jnp.where(lane < N_EXPERTS, raw + br_ref[...], NEG_BIG)
    work = logits
    vals, ids, hots = [], [], []
    for _k in range(TOP_K):
        m = jnp.max(work, axis=-1, keepdims=True)
        idx = jnp.min(jnp.where(work == m, lane, LANES), axis=-1, keepdims=True)
        hot = lane == idx
        vals.append(m)
        ids.append(idx)
        hots.append(hot)
        work = jnp.where(hot, NEG_BIG * 2.0, work)
    es = [jnp.exp(v - vals[0]) for v in vals]
    den = es[0] + es[1] + es[2] + es[3]
    member = jnp.zeros((tm, LANES), F32)
    for hot in hots:
        member = member + jnp.where(hot, 1.0, 0.0)
    ii = lax.broadcasted_iota(I32, (tm, tm), 0)
    jj = lax.broadcasted_iota(I32, (tm, tm), 1)
    strict = jnp.where(ii > jj, 1.0, 0.0).astype(BF)
    carry = carry_ref[0:1, :]
    before = _dot(strict, member.astype(BF)) + carry
    meta = jnp.zeros((tm, LANES), F32)
    for k in range(TOP_K):
        rank = jnp.sum(jnp.where(hots[k], before, 0.0), axis=-1, keepdims=True)
        meta = jnp.where(lane == k, ids[k].astype(F32), meta)
        meta = jnp.where(lane == TOP_K + k, rank, meta)
        meta = jnp.where(lane == 2 * TOP_K + k, es[k] / den, meta)
    meta_ref[...] = meta
    new_carry = carry + jnp.sum(member, axis=0, keepdims=True)
    carry_ref[...] = jnp.broadcast_to(new_carry, carry_ref.shape)
    cnt_ref[...] = jnp.broadcast_to(new_carry, cnt_ref.shape)


def _router(h, norm_w, w_router, b_router):
    n = h.shape[0]
    tm = ROUTER_TM
    wr = jnp.pad(w_router, ((0, 0), (0, LANES - N_EXPERTS)))
    br = jnp.pad(b_router, (0, LANES - N_EXPERTS)).reshape(1, LANES)
    return pl.pallas_call(
        _router_kernel,
        out_shape=(jax.ShapeDtypeStruct((n * ROW_TILES, LANES), F32),
                   jax.ShapeDtypeStruct((n, LANES), F32),
                   jax.ShapeDtypeStruct((SUBLANES, LANES), F32)),
        grid=(n // tm,),
        in_specs=[pl.BlockSpec((tm, D_MODEL), lambda i: (i, 0)), _full((1, D_MODEL)),
                  _full((D_MODEL, LANES)), _full((1, LANES))],
        out_specs=[pl.BlockSpec((tm * ROW_TILES, LANES), lambda i: (i, 0)),
                   pl.BlockSpec((tm, LANES), lambda i: (i, 0)),
                   _full((SUBLANES, LANES))],
        scratch_shapes=[pltpu.VMEM((SUBLANES, LANES), F32)],
        compiler_params=_params(("arbitrary",)),
        name="router",
    )(h, norm_w.reshape(1, -1), wr, br)


def _row(ref, r):
    return ref.at[pl.ds(pl.multiple_of(r * ROW_TILES, ROW_TILES), ROW_TILES)]


def _dispatch_kernel(dest_ref, t_ref, xs_hbm, sem):
    tt = DISPATCH_TT

    def copy(j, k):
        return pltpu.make_async_copy(_row(t_ref, j), _row(xs_hbm, dest_ref[j * TOP_K + k]), sem)

    def issue(j, carry):
        for k in range(TOP_K):
            copy(j, k).start(priority=k % DMA_PRIORITIES)
        return carry

    def drain(j, carry):
        for k in range(TOP_K):
            copy(j, k).wait()
        return carry

    lax.fori_loop(0, tt, issue, 0)
    lax.fori_loop(0, tt, drain, 0)


def _dispatch(t_rows, dest):
    n = t_rows.shape[0] // ROW_TILES
    tt = DISPATCH_TT
    return pl.pallas_call(
        _dispatch_kernel,
        out_shape=jax.ShapeDtypeStruct((n * TOP_K * ROW_TILES, LANES), F32),
        grid=(n // tt,),
        in_specs=[pl.BlockSpec((tt * TOP_K,), lambda i: (i,), memory_space=pltpu.SMEM),
                  pl.BlockSpec((tt * ROW_TILES, LANES), lambda i: (i, 0))],
        out_specs=pl.BlockSpec(memory_space=pl.ANY),
        scratch_shapes=[pltpu.SemaphoreType.DMA],
        compiler_params=_params(("arbitrary",)),
        name="dispatch",
    )(dest, t_rows)


def _expert_kernel(tile_ref, exp_ref, lo_ref, hi_ref, first_ref, newexp_ref, slot_ref, next_ref,
                   xs_ref, wgu_hbm, bgu_ref, wdn_hbm, bdn_ref, ys_ref, wgu_bf, wdn_bf, wgu_f32, wdn_f32, sems):
    w = pl.program_id(0)
    tm = EXPERT_TM
    lo = lo_ref[w]
    hi = hi_ref[w]

    def weight_copies(e, slot):
        return (pltpu.make_async_copy(wgu_hbm.at[e], wgu_f32.at[slot], sems.at[0, slot]),
                pltpu.make_async_copy(wdn_hbm.at[e], wdn_f32.at[slot], sems.at[1, slot]))

    @pl.when(w == 0)
    def _():
        for cp in weight_copies(exp_ref[0], 0):
            cp.start()

    @pl.when(newexp_ref[w] == 1)
    def _():
        slot = slot_ref[w]
        for cp in weight_copies(exp_ref[w], slot):
            cp.wait()
        wgu_bf[...] = wgu_f32[slot].astype(BF)
        wdn_bf[...] = wdn_f32[slot].astype(BF)

        @pl.when(next_ref[w] >= 0)
        def _():
            for cp in weight_copies(next_ref[w], 1 - slot):
                cp.start()

    @pl.when(hi > lo)
    def _():
        x = jnp.concatenate([xs_ref[pl.ds(c, tm, stride=ROW_TILES), :] for c in range(ROW_TILES)],
                            axis=1).astype(BF)
        gu = _dot(x, wgu_bf[...]) + bgu_ref[0]
        g = jnp.minimum(gu[:, :D_EXPERT], SWIGLU_LIMIT)
        up = jnp.clip(gu[:, D_EXPERT:], -SWIGLU_LIMIT, SWIGLU_LIMIT)
        act = (up + 1.0) * (g * jax.nn.sigmoid(SWIGLU_ALPHA * g))
        y = _dot(act.astype(BF), wdn_bf[...]) + bdn_ref[0]
        rows = tile_ref[w] * tm + lax.broadcasted_iota(I32, (tm, LANES), 0)
        mine = (rows >= lo) & (rows < hi)
        is_first = first_ref[w] == 1

        @pl.when(is_first)
        def _():
            for c in range(ROW_TILES):
                ys_ref[pl.ds(c, tm, stride=ROW_TILES), :] = jnp.where(mine, y[:, LANES * c:LANES * (c + 1)], 0.0)

        @pl.when(jnp.logical_not(is_first))
        def _():
            for c in range(ROW_TILES):
                old = ys_ref[pl.ds(c, tm, stride=ROW_TILES), :]
                ys_ref[pl.ds(c, tm, stride=ROW_TILES), :] = jnp.where(mine, y[:, LANES * c:LANES * (c + 1)], old)


def _experts(xs, items, w_gu, b_gu, w_dn, b_dn):
    n_items = items[0].shape[0]
    tm = EXPERT_TM
    grid_spec = pltpu.PrefetchScalarGridSpec(
        num_scalar_prefetch=len(items),
        grid=(n_items,),
        in_specs=[
            pl.BlockSpec((tm * ROW_TILES, LANES), lambda w, t, e, *_: (t[w], 0)),
            pl.BlockSpec(memory_space=pl.ANY),
            pl.BlockSpec((1, 1, 2 * D_EXPERT), lambda w, t, e, *_: (e[w], 0, 0)),
            pl.BlockSpec(memory_space=pl.ANY),
            pl.BlockSpec((1, 1, D_MODEL), lambda w, t, e, *_: (e[w], 0, 0)),
        ],
        out_specs=pl.BlockSpec((tm * ROW_TILES, LANES), lambda w, t, e, *_: (t[w], 0)),
        scratch_shapes=[pltpu.VMEM((D_MODEL, 2 * D_EXPERT), BF), pltpu.VMEM((D_EXPERT, D_MODEL), BF),
                        pltpu.VMEM((2, D_MODEL, 2 * D_EXPERT), F32), pltpu.VMEM((2, D_EXPERT, D_MODEL), F32),
                        pltpu.SemaphoreType.DMA((2, 2))],
    )
    return pl.pallas_call(
        _expert_kernel,
        out_shape=jax.ShapeDtypeStruct(xs.shape, F32),
        grid_spec=grid_spec,
        compiler_params=_params(("arbitrary",)),
        name="experts",
    )(*items, xs, w_gu, b_gu.reshape(N_EXPERTS, 1, -1), w_dn, b_dn.reshape(N_EXPERTS, 1, -1))


def _expert_items(counts, n_rows):
    tm = EXPERT_TM
    n_tiles = n_rows // tm
    n_items = n_tiles + N_EXPERTS - 1
    end = jnp.cumsum(counts)
    off = end - counts
    first_tile = off // tm
    last_tile = jnp.maximum(end - 1, 0) // tm
    per = jnp.where(counts > 0, last_tile - first_tile + 1, 0)
    cum = jnp.cumsum(per)
    start = cum - per
    total = cum[-1]
    w = jnp.arange(n_items, dtype=I32)
    wc = jnp.minimum(w, total - 1)
    e_of = jnp.minimum(jnp.sum((cum[None, :] <= wc[:, None]).astype(I32), axis=1), N_EXPERTS - 1)
    sel = e_of[:, None] == jnp.arange(N_EXPERTS, dtype=I32)
    pick = lambda v: jnp.sum(jnp.where(sel, v[None, :], 0), axis=1)
    t_of = (pick(first_tile) + (wc - pick(start))).astype(I32)
    valid = w < total
    lo = jnp.where(valid, jnp.maximum(pick(off), t_of * tm), 0).astype(I32)
    hi = jnp.where(valid, jnp.minimum(pick(end), (t_of + 1) * tm), 0).astype(I32)
    prev_t = jnp.concatenate([jnp.full((1,), -1, I32), t_of[:-1]])
    first = (t_of != prev_t).astype(I32)
    prev_e = jnp.concatenate([jnp.full((1,), -1, I32), e_of[:-1]])
    newexp = (e_of != prev_e).astype(I32)
    slot = (jnp.cumsum(newexp) - 1) % 2
    later_change = (w[None, :] > w[:, None]) & (newexp[None, :] == 1)
    nxt = jnp.min(jnp.where(later_change, w[None, :], n_items), axis=1)
    next_e = jnp.sum(jnp.where(nxt[:, None] == w[None, :], e_of[None, :], 0), axis=1)
    next_e = jnp.where(nxt < n_items, next_e, -1).astype(I32)
    return t_of, e_of, lo, hi, first, newexp, slot.astype(I32), next_e


def _combine_kernel(dest_ref, meta_ref, h_ref, fw_ref, ys_hbm, o_ref, buf_ref, sems):
    tt = COMBINE_TT
    tp = tt // COMBINE_PHASES

    def copy(j, k, p):
        return pltpu.make_async_copy(_row(ys_hbm, dest_ref[j * TOP_K + k]), _row(buf_ref, k * tt + j), sems.at[p])

    for p in range(COMBINE_PHASES):
        def issue(j, carry, p=p):
            for k in range(TOP_K):
                copy(j, k, p).start(priority=k % DMA_PRIORITIES)
            return carry
        lax.fori_loop(p * tp, (p + 1) * tp, issue, 0)

    for p in range(COMBINE_PHASES):
        def drain(j, carry, p=p):
            for k in range(TOP_K):
                copy(j, k, p).wait()
            return carry
        lax.fori_loop(p * tp, (p + 1) * tp, drain, 0)
        rows = pl.ds(p * tp, tp)
        meta = meta_ref[rows, :]
        wts = [jnp.broadcast_to(meta[:, 2 * TOP_K + k:2 * TOP_K + k + 1], (tp, LANES)) for k in range(TOP_K)]
        cols = []
        for c in range(ROW_TILES):
            acc = h_ref[rows, LANES * c:LANES * (c + 1)]
            for k in range(TOP_K):
                acc = acc + wts[k] * buf_ref[pl.ds((k * tt + p * tp) * ROW_TILES + c, tp, stride=ROW_TILES), :]
            cols.append(acc)
        o_ref[rows, :] = _rms(jnp.concatenate(cols, axis=1), fw_ref[...])


def _combine(dest, meta, h, final_w, ys):
    n = h.shape[0]
    tt = COMBINE_TT
    return pl.pallas_call(
        _combine_kernel,
        out_shape=jax.ShapeDtypeStruct((n, D_MODEL), F32),
        grid=(n // tt,),
        in_specs=[pl.BlockSpec((tt * TOP_K,), lambda i: (i,), memory_space=pltpu.SMEM),
                  pl.BlockSpec((tt, LANES), lambda i: (i, 0)),
                  pl.BlockSpec((tt, D_MODEL), lambda i: (i, 0)),
                  _full((1, D_MODEL)),
                  pl.BlockSpec(memory_space=pl.ANY)],
        out_specs=pl.BlockSpec((tt, D_MODEL), lambda i: (i, 0)),
        scratch_shapes=[pltpu.VMEM((TOP_K * tt * ROW_TILES, LANES), F32),
                        pltpu.SemaphoreType.DMA((COMBINE_PHASES,))],
        compiler_params=_params(("arbitrary",)),
        name="combine",
    )(dest, meta, h, final_w.reshape(1, -1), ys)


def _moe(h, norm_w, w_router, b_router, w_gu, b_gu, w_dn, b_dn, final_w):
    n = h.shape[0]
    t_rows, meta, cnt = _router(h, norm_w, w_router, b_router)
    eid = meta[:, :TOP_K].astype(I32)
    rank = meta[:, TOP_K:2 * TOP_K].astype(I32)
    counts = cnt[0, :N_EXPERTS].astype(I32)
    off = jnp.cumsum(counts) - counts
    hot = eid[..., None] == jnp.arange(N_EXPERTS, dtype=I32)
    dest = (jnp.sum(jnp.where(hot, off, 0), axis=-1) + rank).reshape(-1)
    xs = _dispatch(t_rows, dest)
    ys = _experts(xs, _expert_items(counts, n * TOP_K), w_gu, b_gu, w_dn, b_dn)
    return _combine(dest, meta, h, final_w, ys)


def kernel(x, mem, norm_mix_w, w_in, w_gate, b_gate, attn_sinks, conv_w, conv_b, dt_bias, a_log, d_skip,
           ssm_norm_w, w_attn_o, w_ssm_o, w_out, norm_cross_w, norm_mem_w, w_cq, w_ckv, w_co,
           norm_ffn_w, w_router, b_router, w_gu, b_gu, w_dn, b_dn, final_norm_w):
    bsz, seq, _ = x.shape
    assert norm_mix_w.shape[0] == 1, "single-layer block"
    x2 = x.reshape(bsz * seq, D_MODEL)
    q, kv, z, xbc, dt_raw, gates = _proj(x2, norm_mix_w[0], w_in[0], w_gate[0], b_gate[0])
    attn = _swa(q, kv, attn_sinks[0], bsz, seq)
    ssm = _ssd(xbc, z, dt_raw, conv_w[0], conv_b[0], dt_bias[0], a_log[0], d_skip[0], ssm_norm_w[0], bsz, seq)
    h = _outproj(x2, attn, ssm, gates, w_attn_o[0], w_ssm_o[0], w_out[0])
    mkv = _memkv(mem.reshape(bsz * MEM_TOKENS, D_MODEL), norm_mem_w[0], w_ckv[0], bsz)
    h = _cross(h, mkv, norm_cross_w[0], w_cq[0], w_co[0], seq)
    out = _moe(h, norm_ffn_w[0], w_router[0], b_router[0], w_gu[0], b_gu[0], w_dn[0], b_dn[0], final_norm_w)
    return out.reshape(bsz, seq, D_MODEL)
```

```python
import functools

import jax
import jax.numpy as jnp
from jax import lax
from jax.experimental import pallas as pl
from jax.experimental.pallas import tpu as pltpu

BF = jnp.bfloat16
F32 = jnp.float32
I32 = jnp.int32

D_MODEL = 1024
RMS_EPS = 1e-5
ATTN_HEADS = 16
ATTN_KV_HEADS = 2
HEAD_DIM = 64
WINDOW = 128
Q_DIM = ATTN_HEADS * HEAD_DIM
KV_DIM = ATTN_KV_HEADS * HEAD_DIM
D_INNER = 2 * D_MODEL
SSM_HEADS = 32
SSM_GROUPS = 4
D_STATE = 128
CONV_WIDTH = 4
CHUNK = 128
XBC_DIM = D_INNER + 2 * SSM_GROUPS * D_STATE
MEM_TOKENS = 256
CROSS_HEADS = 4
CROSS_HEAD_DIM = D_MODEL // CROSS_HEADS
N_EXPERTS = 32
TOP_K = 4
D_EXPERT = D_MODEL
SWIGLU_LIMIT = 7.0
SWIGLU_ALPHA = 1.702

LANES = 128
SUBLANES = 8
ROW_TILES = D_MODEL // LANES
NEG_BIG = -1e30
VMEM_LIMIT = 56 * 1024 * 1024

PROJ_TM = 256
DMA_PRIORITIES = 2
OUT_TM = 512
CROSS_TM = 512
ROUTER_TM = 512
DISPATCH_TT = 512
EXPERT_TM = 512
EXPERT_SUB = 256
COMBINE_TT = 512
COMBINE_PHASES = 2


def _rms(x, w):
    return x * lax.rsqrt(jnp.mean(x * x, axis=-1, keepdims=True) + RMS_EPS) * w


def _dot(a, b):
    return jnp.dot(a, b, preferred_element_type=F32)


def _dot_nt(a, b):
    return lax.dot_general(a, b, (((1,), (1,)), ((), ())), preferred_element_type=F32)


def _dot_f32(a, b):
    return jnp.dot(a, b, preferred_element_type=F32, precision=lax.Precision.HIGHEST)


def _params(sem):
    return pltpu.CompilerParams(dimension_semantics=sem, vmem_limit_bytes=VMEM_LIMIT)


def _full(shape):
    nd = len(shape)
    return pl.BlockSpec(shape, lambda *_: (0,) * nd)


def _proj_kernel(x_ref, nw_ref, wq_ref, wkv_ref, wz_ref, wxbc_ref, wdt_ref, wg_ref, bg_ref,
                 q_ref, kv_ref, z_ref, xbc_ref, dt_ref, g_ref):
    u = _rms(x_ref[...], nw_ref[...]).astype(BF)
    q_ref[...] = _dot(u, wq_ref[...]).astype(BF)
    kv_ref[...] = _dot(u, wkv_ref[...]).astype(BF)
    z_ref[...] = _dot(u, wz_ref[...]).astype(BF)
    xbc_ref[...] = _dot(u, wxbc_ref[...]).astype(BF)
    dt_ref[...] = _dot(u, wdt_ref[...])
    g_ref[...] = jax.nn.sigmoid(_dot(u, wg_ref[...]) + bg_ref[...]).astype(BF)


def _proj(x2, norm_w, w_in, w_gate, b_gate):
    n = x2.shape[0]
    tm = PROJ_TM
    s0, s1, s2, s3, s4 = Q_DIM, Q_DIM + KV_DIM, Q_DIM + 2 * KV_DIM, Q_DIM + 2 * KV_DIM + D_INNER, \
        Q_DIM + 2 * KV_DIM + D_INNER + XBC_DIM
    wq = (w_in[:, :s0] * (HEAD_DIM ** -0.5)).astype(BF)
    wk, wv = w_in[:, s0:s1], w_in[:, s1:s2]
    dup = lambda w: jnp.concatenate([w[:, :HEAD_DIM], w[:, :HEAD_DIM], w[:, HEAD_DIM:], w[:, HEAD_DIM:]], axis=1)
    wkv = jnp.concatenate([dup(wk), dup(wv)], axis=1).astype(BF)
    wz = w_in[:, s2:s3].astype(BF)
    wxbc = w_in[:, s3:s4].astype(BF)
    wdt = jnp.pad(w_in[:, s4:], ((0, 0), (0, LANES - SSM_HEADS))).astype(BF)
    wg = w_gate.astype(BF)
    outs = (
        jax.ShapeDtypeStruct((n, Q_DIM), BF),
        jax.ShapeDtypeStruct((n, 4 * LANES), BF),
        jax.ShapeDtypeStruct((n, D_INNER), BF),
        jax.ShapeDtypeStruct((n, XBC_DIM), BF),
        jax.ShapeDtypeStruct((n, LANES), F32),
        jax.ShapeDtypeStruct((n, 2 * D_MODEL), BF),
    )
    row = lambda w: pl.BlockSpec((tm, w), lambda i: (i, 0))
    return pl.pallas_call(
        _proj_kernel,
        out_shape=outs,
        grid=(n // tm,),
        in_specs=[row(D_MODEL), _full((1, D_MODEL)), _full(wq.shape), _full(wkv.shape), _full(wz.shape),
                  _full(wxbc.shape), _full(wdt.shape), _full(wg.shape), _full((1, 2 * D_MODEL))],
        out_specs=[row(Q_DIM), row(4 * LANES), row(D_INNER), row(XBC_DIM), row(LANES), row(2 * D_MODEL)],
        compiler_params=_params(("parallel",)),
        name="proj",
    )(x2, norm_w.reshape(1, -1), wq, wkv, wz, wxbc, wdt, wg, b_gate.reshape(1, -1))


PAIRS_PER_KV = ATTN_HEADS // ATTN_KV_HEADS // 2


SWA_BLOCKS = 4


def _swa_block(sink_ref, q_ref, kv_prev, kv_cur, bias, o_ref):
    w = WINDOW
    upper = lax.broadcasted_iota(I32, (2 * w, LANES), 1) >= HEAD_DIM
    zero = jnp.zeros((2 * w, LANES), BF)
    ones = jnp.ones((2 * w, LANES), BF)
    chains = [(kvh, r) for kvh in range(ATTN_KV_HEADS) for r in range(2)]
    half = lambda t, r: jnp.where(upper, t, zero) if r else jnp.where(upper, zero, t)
    scores, values, sinks = [], [], []
    for kvh, r in chains:
        ks = slice(LANES * kvh, LANES * (kvh + 1))
        vs = slice(LANES * (ATTN_KV_HEADS + kvh), LANES * (ATTN_KV_HEADS + kvh + 1))
        k2 = jnp.concatenate([kv_prev[:, ks], kv_cur[:, ks]], axis=0)
        v2 = jnp.concatenate([kv_prev[:, vs], kv_cur[:, vs]], axis=0)
        q4 = jnp.concatenate([q_ref[:, LANES * (kvh * PAIRS_PER_KV + p):LANES * (kvh * PAIRS_PER_KV + p + 1)]
                              for p in range(PAIRS_PER_KV)], axis=0)
        scores.append(_dot_nt(q4, half(k2, r)) + bias(kvh, r))
        values.append(jnp.concatenate([half(v2, r), ones], axis=1))
        sinks.append(jnp.concatenate(
            [jnp.full((w, 1), sink_ref[2 * (kvh * PAIRS_PER_KV + p) + r], F32) for p in range(PAIRS_PER_KV)],
            axis=0))
    maxes = [jnp.maximum(jnp.max(s, axis=-1, keepdims=True), sk) for s, sk in zip(scores, sinks)]
    probs = [jnp.exp(s - m).astype(BF) for s, m in zip(scores, maxes)]
    nds = [_dot(p, v) for p, v in zip(probs, values)]
    outs = [nd[:, :LANES] / (nd[:, LANES:] + jnp.exp(sk - m)) for nd, sk, m in zip(nds, sinks, maxes)]
    for kvh in range(ATTN_KV_HEADS):
        o4 = outs[2 * kvh] + outs[2 * kvh + 1]
        for p in range(PAIRS_PER_KV):
            pidx = kvh * PAIRS_PER_KV + p
            o_ref[:, LANES * pidx:LANES * (pidx + 1)] = o4[w * p:w * (p + 1)].astype(BF)


def _swa_kernel(sink_ref, q_ref, kvc_ref, kvp_ref, bias_ref, o_ref):
    has_prev = jnp.minimum(pl.program_id(1), 1)
    for sub in range(SWA_BLOCKS):
        rows = pl.ds(WINDOW * sub, WINDOW)
        kv_prev = kvp_ref if sub == 0 else kvc_ref.at[pl.ds(WINDOW * (sub - 1), WINDOW)]
        variant = has_prev if sub == 0 else 1
        _swa_block(sink_ref, q_ref.at[rows], kv_prev, kvc_ref.at[rows],
                   lambda kvh, r, v=variant: bias_ref[v, kvh, r], o_ref.at[rows])


def _swa_bias():
    qi = jnp.arange(WINDOW)[:, None]
    kj = jnp.arange(2 * WINDOW)[None, :]
    dist = qi + WINDOW - kj
    in_window = (dist >= 0) & (dist < WINDOW)
    heads = jnp.arange(ATTN_HEADS).reshape(ATTN_KV_HEADS, PAIRS_PER_KV, 2).transpose(0, 2, 1)
    slopes = jnp.exp2(-8.0 * (heads + 1).astype(F32) / ATTN_HEADS)
    bias = -slopes[..., None, None] * dist.astype(F32)
    tabs = []
    for has_prev in (False, True):
        mask = in_window & (has_prev | (kj >= WINDOW))
        tabs.append(jnp.where(mask, bias, NEG_BIG).reshape(ATTN_KV_HEADS, 2, PAIRS_PER_KV * WINDOW, 2 * WINDOW))
    return jnp.stack(tabs)


def _swa(q, kv, sinks, bsz, seq):
    nb = seq // (WINDOW * SWA_BLOCKS)
    n = bsz * seq
    rows = WINDOW * SWA_BLOCKS
    return pl.pallas_call(
        _swa_kernel,
        out_shape=jax.ShapeDtypeStruct((n, Q_DIM), BF),
        grid=(bsz, nb),
        in_specs=[
            pl.BlockSpec(memory_space=pltpu.SMEM),
            pl.BlockSpec((rows, Q_DIM), lambda b, j: (b * nb + j, 0)),
            pl.BlockSpec((rows, 4 * LANES), lambda b, j: (b * nb + j, 0)),
            pl.BlockSpec((WINDOW, 4 * LANES), lambda b, j: (jnp.maximum((b * nb + j) * SWA_BLOCKS - 1, 0), 0)),
            _full((2, ATTN_KV_HEADS, 2, PAIRS_PER_KV * WINDOW, 2 * WINDOW)),
        ],
        out_specs=pl.BlockSpec((rows, Q_DIM), lambda b, j: (b * nb + j, 0)),
        compiler_params=_params(("parallel", "parallel")),
        name="swa",
    )(sinks.astype(F32), q, kv, kv, _swa_bias())


def _softplus(x):
    return jnp.maximum(x, 0.0) + jnp.log1p(jnp.exp(-jnp.abs(x)))


SSD_CHUNKS = 2


def _ssd_chunk(xbc_ref, z_ref, dtc_ref, dt_row_raw, cw_ref, cb_ref, dtbc_ref, dtbr_ref, alc_ref, alr_ref,
               dskip_ref, nw_ref, o_ref, state_ref, tail_ref):
    L = CHUNK
    x_raw = xbc_ref[...].astype(F32)
    prev8 = tail_ref[...]
    row8 = lax.broadcasted_iota(I32, (SUBLANES, XBC_DIM), 0)
    acc = x_raw * cw_ref[CONV_WIDTH - 1:CONV_WIDTH, :] + cb_ref[...]
    for s in range(1, CONV_WIDTH):
        sh = pltpu.roll(x_raw, s, axis=0)
        top = jnp.where(row8 < s, pltpu.roll(prev8, s, axis=0), sh[:SUBLANES])
        sh = jnp.concatenate([top, sh[SUBLANES:]], axis=0)
        acc = acc + sh * cw_ref[CONV_WIDTH - 1 - s:CONV_WIDTH - s, :]
    tail_ref[...] = x_raw[L - SUBLANES:, :]
    xc = acc * jax.nn.sigmoid(acc)

    dt_col = _softplus(dtc_ref[...] + dtbc_ref[...])
    dt_row = _softplus(dt_row_raw + dtbr_ref[...])
    a_col = dt_col * (-jnp.exp(alc_ref[...]))
    a_row = dt_row * (-jnp.exp(alr_ref[...]))
    ii = lax.broadcasted_iota(I32, (L, L), 0)
    jj = lax.broadcasted_iota(I32, (L, L), 1)
    causal = ii >= jj
    tri_l = jnp.where(causal, 1.0, 0.0).astype(F32)
    tri_u = jnp.where(ii <= jj, 1.0, 0.0).astype(F32)
    acum_col = _dot_f32(tri_l, a_col)
    acum_row = _dot_f32(a_row, tri_u)
    decay_col = jnp.exp(acum_col)
    last_col = decay_col[L - 1:L, :]
    last_row = acum_row[:, L - 1:L]
    wrow = jnp.exp(last_row - acum_row) * dt_row
    shifted_row = acum_row - jnp.log(dt_row)

    upper = lax.broadcasted_iota(I32, (L, LANES), 1) >= 64
    upper1 = upper[:1]
    lower = jnp.logical_not(upper)
    pairs_per_group = SSM_HEADS // SSM_GROUPS // 2
    group_w = D_INNER // SSM_GROUPS
    for g in range(SSM_GROUPS):
        b_g = xc[:, D_INNER + D_STATE * g:D_INNER + D_STATE * (g + 1)]
        c_g = xc[:, D_INNER + D_STATE * (SSM_GROUPS + g):D_INNER + D_STATE * (SSM_GROUPS + g + 1)]
        c_bf = c_g.astype(BF)
        cb = _dot_nt(c_bf, b_g.astype(BF))
        b_gt = b_g.astype(F32).T
        pairs = range(g * pairs_per_group, (g + 1) * pairs_per_group)
        states = [state_ref[p] for p in pairs]
        y_off = _dot(c_bf, jnp.concatenate(states, axis=1).astype(BF))
        ys = []
        for j, pidx in enumerate(pairs):
            x_pair = xc[:, LANES * pidx:LANES * (pidx + 1)]
            xms = (jnp.where(lower, x_pair, 0.0).astype(BF), jnp.where(upper, x_pair, 0.0).astype(BF))
            acc_y = None
            acc_s = None
            for r in range(2):
                h = 2 * pidx + r
                acol = jnp.broadcast_to(acum_col[:, h:h + 1], (L, L))
                m_h = cb * jnp.exp(jnp.where(causal, acol - shifted_row[h:h + 1, :], NEG_BIG))
                y = _dot(m_h.astype(BF), xms[r])
                acc_y = y if acc_y is None else acc_y + y
                bw = (b_gt * wrow[h:h + 1, :]).astype(BF)
                sn = _dot(bw, xms[r])
                acc_s = sn if acc_s is None else acc_s + sn
            h0, h1 = 2 * pidx, 2 * pidx + 1
            dec = jnp.where(upper, jnp.broadcast_to(decay_col[:, h1:h1 + 1], (L, LANES)),
                            jnp.broadcast_to(decay_col[:, h0:h0 + 1], (L, LANES)))
            d0 = jnp.broadcast_to(last_col[:, h0:h0 + 1], (1, LANES))
            d1 = jnp.broadcast_to(last_col[:, h1:h1 + 1], (1, LANES))
            state_ref[pidx] = states[j] * jnp.where(upper1, d1, d0) + acc_s
            ys.append(acc_y + y_off[:, LANES * j:LANES * (j + 1)] * dec
                      + x_pair * dskip_ref[:, LANES * pidx:LANES * (pidx + 1)])
        y_g = jnp.concatenate(ys, axis=1)
        z_g = z_ref[:, group_w * g:group_w * (g + 1)].astype(F32)
        y_g = y_g * (z_g * jax.nn.sigmoid(z_g))
        y_g = _rms(y_g, nw_ref[:, group_w * g:group_w * (g + 1)])
        o_ref[:, group_w * g:group_w * (g + 1)] = y_g.astype(BF)


def _ssd_kernel(xbc_ref, z_ref, dtc_ref, dtr_ref, cw_ref, cb_ref, dtbc_ref, dtbr_ref, alc_ref, alr_ref,
                dskip_ref, nw_ref, o_ref, state_ref, tail_ref):
    @pl.when(pl.program_id(1) == 0)
    def _():
        state_ref[...] = jnp.zeros_like(state_ref)
        tail_ref[...] = jnp.zeros_like(tail_ref)

    for sub in range(SSD_CHUNKS):
        rows = pl.ds(CHUNK * sub, CHUNK)
        _ssd_chunk(xbc_ref.at[rows], z_ref.at[rows], dtc_ref.at[rows], dtr_ref[:, CHUNK * sub:CHUNK * (sub + 1)],
                   cw_ref, cb_ref, dtbc_ref, dtbr_ref, alc_ref, alr_ref, dskip_ref, nw_ref, o_ref.at[rows],
                   state_ref, tail_ref)


def _ssd(xbc, z, dt_raw, conv_w, conv_b, dt_bias, a_log, d_skip, norm_w, bsz, seq):
    rows = CHUNK * SSD_CHUNKS
    nc = seq // rows
    n = bsz * seq
    pad = LANES - SSM_HEADS
    dt_row = dt_raw[:, :SSM_HEADS].T
    dtb_col = jnp.pad(dt_bias, (0, pad)).reshape(1, LANES)
    al_col = jnp.pad(a_log, (0, pad)).reshape(1, LANES)
    dtb_row = jnp.broadcast_to(dt_bias[:, None], (SSM_HEADS, CHUNK))
    al_row = jnp.broadcast_to(a_log[:, None], (SSM_HEADS, CHUNK))
    dskip = jnp.repeat(d_skip, D_INNER // SSM_HEADS).reshape(1, D_INNER)
    blk = lambda w: pl.BlockSpec((rows, w), lambda b, j: (b * nc + j, 0))
    return pl.pallas_call(
        _ssd_kernel,
        out_shape=jax.ShapeDtypeStruct((n, D_INNER), BF),
        grid=(bsz, nc),
        in_specs=[
            blk(XBC_DIM), blk(D_INNER), blk(LANES),
            pl.BlockSpec((SSM_HEADS, rows), lambda b, j: (0, b * nc + j)),
            _full((CONV_WIDTH, XBC_DIM)), _full((1, XBC_DIM)),
            _full((1, LANES)), _full((SSM_HEADS, CHUNK)), _full((1, LANES)), _full((SSM_HEADS, CHUNK)),
            _full((1, D_INNER)), _full((1, D_INNER)),
        ],
        out_specs=blk(D_INNER),
        scratch_shapes=[pltpu.VMEM((SSM_HEADS // 2, D_STATE, LANES), F32),
                        pltpu.VMEM((SUBLANES, XBC_DIM), F32)],
        compiler_params=_params(("parallel", "arbitrary")),
        name="ssd",
    )(xbc, z, dt_raw, dt_row, conv_w, conv_b.reshape(1, -1), dtb_col, dtb_row, al_col, al_row,
      dskip, norm_w.reshape(1, -1))


def _outproj_kernel(x_ref, attn_ref, ssm_ref, g_ref, wa_ref, ws_ref, wo_ref, h_ref):
    a = _dot(attn_ref[...], wa_ref[...])
    s = _dot(ssm_ref[...], ws_ref[...])
    merged = g_ref[:, :D_MODEL].astype(F32) * a + g_ref[:, D_MODEL:].astype(F32) * s
    h_ref[...] = x_ref[...] + _dot(merged.astype(BF), wo_ref[...])


def _outproj(x2, attn, ssm, gates, w_attn_o, w_ssm_o, w_out):
    n = x2.shape[0]
    tm = OUT_TM
    row = lambda w: pl.BlockSpec((tm, w), lambda i: (i, 0))
    return pl.pallas_call(
        _outproj_kernel,
        out_shape=jax.ShapeDtypeStruct((n, D_MODEL), F32),
        grid=(n // tm,),
        in_specs=[row(D_MODEL), row(Q_DIM), row(D_INNER), row(2 * D_MODEL),
                  _full((Q_DIM, D_MODEL)), _full((D_INNER, D_MODEL)), _full((D_MODEL, D_MODEL))],
        out_specs=row(D_MODEL),
        compiler_params=_params(("parallel",)),
        name="outproj",
    )(x2, attn, ssm, gates, w_attn_o.astype(BF), w_ssm_o.astype(BF), w_out.astype(BF))


def _memkv_kernel(mem_ref, nw_ref, w_ref, kv_ref):
    mn = _rms(mem_ref[...], nw_ref[...]).astype(BF)
    kv_ref[...] = _dot(mn, w_ref[...]).astype(BF)


def _memkv(mem2, norm_w, w_ckv, bsz):
    return pl.pallas_call(
        _memkv_kernel,
        out_shape=jax.ShapeDtypeStruct((bsz * MEM_TOKENS, 2 * D_MODEL), BF),
        grid=(bsz,),
        in_specs=[pl.BlockSpec((MEM_TOKENS, D_MODEL), lambda b: (b, 0)), _full((1, D_MODEL)),
                  _full((D_MODEL, 2 * D_MODEL))],
        out_specs=pl.BlockSpec((MEM_TOKENS, 2 * D_MODEL), lambda b: (b, 0)),
        compiler_params=_params(("parallel",)),
        name="memkv",
    )(mem2, norm_w.reshape(1, -1), w_ckv.astype(BF))


def _cross_kernel(h_ref, kv_ref, nw_ref, wq_ref, wo_ref, o_ref):
    h = h_ref[...]
    hn = _rms(h, nw_ref[...]).astype(BF)
    q = (_dot(hn, wq_ref[...]) * (CROSS_HEAD_DIM ** -0.5)).astype(BF)
    outs = []
    for hd in range(CROSS_HEADS):
        lo, hi = CROSS_HEAD_DIM * hd, CROSS_HEAD_DIM * (hd + 1)
        s = _dot_nt(q[:, lo:hi], kv_ref[:, lo:hi])
        p = jnp.exp(s - jnp.max(s, axis=-1, keepdims=True))
        den = jnp.sum(p, axis=-1, keepdims=True)
        outs.append((_dot(p.astype(BF), kv_ref[:, D_MODEL + lo:D_MODEL + hi]) / den).astype(BF))
    o = jnp.concatenate(outs, axis=1)
    o_ref[...] = h + _dot(o, wo_ref[...])


def _cross(h, kv, norm_w, w_cq, w_co, seq):
    n = h.shape[0]
    tm = CROSS_TM
    per_b = seq // tm
    row = pl.BlockSpec((tm, D_MODEL), lambda i: (i, 0))
    return pl.pallas_call(
        _cross_kernel,
        out_shape=jax.ShapeDtypeStruct((n, D_MODEL), F32),
        grid=(n // tm,),
        in_specs=[row, pl.BlockSpec((MEM_TOKENS, 2 * D_MODEL), lambda i: (i // per_b, 0)),
                  _full((1, D_MODEL)), _full((D_MODEL, D_MODEL)), _full((D_MODEL, D_MODEL))],
        out_specs=row,
        compiler_params=_params(("parallel",)),
        name="cross",
    )(h, kv, norm_w.reshape(1, -1), w_cq.astype(BF), w_co.astype(BF))


def _router_kernel(h_ref, nw_ref, wr_ref, br_ref, t_ref, meta_ref, cnt_ref, carry_ref):
    i = pl.program_id(0)
    tm = ROUTER_TM

    @pl.when(i == 0)
    def _():
        carry_ref[...] = jnp.zeros_like(carry_ref)

    t = _rms(h_ref[...], nw_ref[...])
    for c in range(ROW_TILES):
        t_ref[pl.ds(c, tm, stride=ROW_TILES), :] = t[:, LANES * c:LANES * (c + 1)]
    lane = lax.broadcasted_iota(I32, (tm, LANES), 1)
    t_hi = t.astype(BF)
    t_lo = (t - t_hi.astype(F32)).astype(BF)
    w_hi = wr_ref[...].astype(BF)
    w_lo = (wr_ref[...] - w_hi.astype(F32)).astype(BF)
    raw = _dot(t_hi, w_hi) + (_dot(t_lo, w_hi) + _dot(t_hi, w_lo))
    logits = jnp.where(lane < N_EXPERTS, raw + br_ref[...], NEG_BIG)
    work = logits
    vals, ids, hots = [], [], []
    for _k in range(TOP_K):
        m = jnp.max(work, axis=-1, keepdims=True)
        idx = jnp.min(jnp.where(work == m, lane, LANES), axis=-1, keepdims=True)
        hot = lane == idx
        vals.append(m)
        ids.append(idx)
        hots.append(hot)
        work = jnp.where(hot, NEG_BIG * 2.0, work)
    es = [jnp.exp(v - vals[0]) for v in vals]
    den = es[0] + es[1] + es[2] + es[3]
    member = jnp.zeros((tm, LANES), F32)
    for hot in hots:
        member = member + jnp.where(hot, 1.0, 0.0)
    ii = lax.broadcasted_iota(I32, (tm, tm), 0)
    jj = lax.broadcasted_iota(I32, (tm, tm), 1)
    strict = jnp.where(ii > jj, 1.0, 0.0).astype(BF)
    carry = carry_ref[0:1, :]
    before = _dot(strict, member.astype(BF)) + carry
    meta = jnp.zeros((tm, LANES), F32)
    for k in range(TOP_K):
        rank = jnp.sum(jnp.where(hots[k], before, 0.0), axis=-1, keepdims=True)
        meta = jnp.where(lane == k, ids[k].astype(F32), meta)
        meta = jnp.where(lane == TOP_K + k, rank, meta)
        meta = jnp.where(lane == 2 * TOP_K + k, es[k] / den, meta)
    meta_ref[...] = meta
    new_carry = carry + jnp.sum(member, axis=0, keepdims=True)
    carry_ref[...] = jnp.broadcast_to(new_carry, carry_ref.shape)
    cnt_ref[...] = jnp.broadcast_to(new_carry, cnt_ref.shape)


def _router(h, norm_w, w_router, b_router):
    n = h.shape[0]
    tm = ROUTER_TM
    wr = jnp.pad(w_router, ((0, 0), (0, LANES - N_EXPERTS)))
    br = jnp.pad(b_router, (0, LANES - N_EXPERTS)).reshape(1, LANES)
    return pl.pallas_call(
        _router_kernel,
        out_shape=(jax.ShapeDtypeStruct((n * ROW_TILES, LANES), F32),
                   jax.ShapeDtypeStruct((n, LANES), F32),
                   jax.ShapeDtypeStruct((SUBLANES, LANES), F32)),
        grid=(n // tm,),
        in_specs=[pl.BlockSpec((tm, D_MODEL), lambda i: (i, 0)), _full((1, D_MODEL)),
                  _full((D_MODEL, LANES)), _full((1, LANES))],
        out_specs=[pl.BlockSpec((tm * ROW_TILES, LANES), lambda i: (i, 0)),
                   pl.BlockSpec((tm, LANES), lambda i: (i, 0)),
                   _full((SUBLANES, LANES))],
        scratch_shapes=[pltpu.VMEM((SUBLANES, LANES), F32)],
        compiler_params=_params(("arbitrary",)),
        name="router",
    )(h, norm_w.reshape(1, -1), wr, br)


def _row(ref, r):
    return ref.at[pl.ds(pl.multiple_of(r * ROW_TILES, ROW_TILES), ROW_TILES)]


def _dispatch_kernel(dest_ref, t_ref, xs_hbm, sem):
    tt = DISPATCH_TT

    def copy(j, k):
        return pltpu.make_async_copy(_row(t_ref, j), _row(xs_hbm, dest_ref[j * TOP_K + k]), sem)

    def issue(j, carry):
        for k in range(TOP_K):
            copy(j, k).start(priority=k % DMA_PRIORITIES)
        return carry

    def drain(j, carry):
        for k in range(TOP_K):
            copy(j, k).wait()
        return carry

    lax.fori_loop(0, tt, issue, 0)
    lax.fori_loop(0, tt, drain, 0)


def _dispatch(t_rows, dest):
    n = t_rows.shape[0] // ROW_TILES
    tt = DISPATCH_TT
    return pl.pallas_call(
        _dispatch_kernel,
        out_shape=jax.ShapeDtypeStruct((n * TOP_K * ROW_TILES, LANES), F32),
        grid=(n // tt,),
        in_specs=[pl.BlockSpec((tt * TOP_K,), lambda i: (i,), memory_space=pltpu.SMEM),
                  pl.BlockSpec((tt * ROW_TILES, LANES), lambda i: (i, 0))],
        out_specs=pl.BlockSpec(memory_space=pl.ANY),
        scratch_shapes=[pltpu.SemaphoreType.DMA],
        compiler_params=_params(("arbitrary",)),
        name="dispatch",
    )(dest, t_rows)


def _expert_kernel(tile_ref, exp_ref, lo_ref, hi_ref, first_ref, newexp_ref, slot_ref, next_ref,
                   xs_ref, wgu_hbm, bgu_ref, wdn_hbm, bdn_ref, ys_ref, wgu_bf, wdn_bf, wgu_f32, wdn_f32, sems):
    w = pl.program_id(0)
    tm = EXPERT_TM
    lo = lo_ref[w]
    hi = hi_ref[w]

    def weight_copies(e, slot):
        return (pltpu.make_async_copy(wgu_hbm.at[e], wgu_f32.at[slot], sems.at[0, slot]),
                pltpu.make_async_copy(wdn_hbm.at[e], wdn_f32.at[slot], sems.at[1, slot]))

    @pl.when(w == 0)
    def _():
        for cp in weight_copies(exp_ref[0], 0):
            cp.start()

    @pl.when(newexp_ref[w] == 1)
    def _():
        slot = slot_ref[w]
        for cp in weight_copies(exp_ref[w], slot):
            cp.wait()
        wgu_bf[...] = wgu_f32[slot].astype(BF)
        wdn_bf[...] = wdn_f32[slot].astype(BF)

        @pl.when(next_ref[w] >= 0)
        def _():
            for cp in weight_copies(next_ref[w], 1 - slot):
                cp.start()

    sb = EXPERT_SUB
    for part in range(tm // sb):
        start = tile_ref[w] * tm + part * sb

        @pl.when(jnp.minimum(hi, start + sb) > jnp.maximum(lo, start))
        def _(part=part, start=start):
            chunk = lambda c: pl.ds(part * sb * ROW_TILES + c, sb, stride=ROW_TILES)
            x = jnp.concatenate([xs_ref[chunk(c), :] for c in range(ROW_TILES)], axis=1).astype(BF)
            gu = _dot(x, wgu_bf[...]) + bgu_ref[0]
            g = jnp.minimum(gu[:, :D_EXPERT], SWIGLU_LIMIT)
            up = jnp.clip(gu[:, D_EXPERT:], -SWIGLU_LIMIT, SWIGLU_LIMIT)
            act = (up + 1.0) * (g * jax.nn.sigmoid(SWIGLU_ALPHA * g))
            y = _dot(act.astype(BF), wdn_bf[...]) + bdn_ref[0]
            rows = start + lax.broadcasted_iota(I32, (sb, LANES), 0)
            mine = (rows >= lo) & (rows < hi)
            is_first = lo <= start

            @pl.when(is_first)
            def _():
                for c in range(ROW_TILES):
                    ys_ref[chunk(c), :] = jnp.where(mine, y[:, LANES * c:LANES * (c + 1)], 0.0)

            @pl.when(jnp.logical_not(is_first))
            def _():
                for c in range(ROW_TILES):
                    ys_ref[chunk(c), :] = jnp.where(mine, y[:, LANES * c:LANES * (c + 1)], ys_ref[chunk(c), :])


def _experts(xs, items, w_gu, b_gu, w_dn, b_dn):
    n_items = items[0].shape[0]
    tm = EXPERT_TM
    grid_spec = pltpu.PrefetchScalarGridSpec(
        num_scalar_prefetch=len(items),
        grid=(n_items,),
        in_specs=[
            pl.BlockSpec((tm * ROW_TILES, LANES), lambda w, t, e, *_: (t[w], 0)),
            pl.BlockSpec(memory_space=pl.ANY),
            pl.BlockSpec((1, 1, 2 * D_EXPERT), lambda w, t, e, *_: (e[w], 0, 0)),
            pl.BlockSpec(memory_space=pl.ANY),
            pl.BlockSpec((1, 1, D_MODEL), lambda w, t, e, *_: (e[w], 0, 0)),
        ],
        out_specs=pl.BlockSpec((tm * ROW_TILES, LANES), lambda w, t, e, *_: (t[w], 0)),
        scratch_shapes=[pltpu.VMEM((D_MODEL, 2 * D_EXPERT), BF), pltpu.VMEM((D_EXPERT, D_MODEL), BF),
                        pltpu.VMEM((2, D_MODEL, 2 * D_EXPERT), F32), pltpu.VMEM((2, D_EXPERT, D_MODEL), F32),
                        pltpu.SemaphoreType.DMA((2, 2))],
    )
    return pl.pallas_call(
        _expert_kernel,
        out_shape=jax.ShapeDtypeStruct(xs.shape, F32),
        grid_spec=grid_spec,
        compiler_params=_params(("arbitrary",)),
        name="experts",
    )(*items, xs, w_gu, b_gu.reshape(N_EXPERTS, 1, -1), w_dn, b_dn.reshape(N_EXPERTS, 1, -1))


def _expert_items(counts, n_rows):
    tm = EXPERT_TM
    n_tiles = n_rows // tm
    n_items = n_tiles + N_EXPERTS - 1
    end = jnp.cumsum(counts)
    off = end - counts
    first_tile = off // tm
    last_tile = jnp.maximum(end - 1, 0) // tm
    per = jnp.where(counts > 0, last_tile - first_tile + 1, 0)
    cum = jnp.cumsum(per)
    start = cum - per
    total = cum[-1]
    w = jnp.arange(n_items, dtype=I32)
    wc = jnp.minimum(w, total - 1)
    e_of = jnp.minimum(jnp.sum((cum[None, :] <= wc[:, None]).astype(I32), axis=1), N_EXPERTS - 1)
    sel = e_of[:, None] == jnp.arange(N_EXPERTS, dtype=I32)
    pick = lambda v: jnp.sum(jnp.where(sel, v[None, :], 0), axis=1)
    t_of = (pick(first_tile) + (wc - pick(start))).astype(I32)
    valid = w < total
    lo = jnp.where(valid, jnp.maximum(pick(off), t_of * tm), 0).astype(I32)
    hi = jnp.where(valid, jnp.minimum(pick(end), (t_of + 1) * tm), 0).astype(I32)
    prev_t = jnp.concatenate([jnp.full((1,), -1, I32), t_of[:-1]])
    first = (t_of != prev_t).astype(I32)
    prev_e = jnp.concatenate([jnp.full((1,), -1, I32), e_of[:-1]])
    newexp = (e_of != prev_e).astype(I32)
    slot = (jnp.cumsum(newexp) - 1) % 2
    later_change = (w[None, :] > w[:, None]) & (newexp[None, :] == 1)
    nxt = jnp.min(jnp.where(later_change, w[None, :], n_items), axis=1)
    next_e = jnp.sum(jnp.where(nxt[:, None] == w[None, :], e_of[None, :], 0), axis=1)
    next_e = jnp.where(nxt < n_items, next_e, -1).astype(I32)
    return t_of, e_of, lo, hi, first, newexp, slot.astype(I32), next_e


def _combine_kernel(dest_ref, meta_ref, h_ref, fw_ref, ys_hbm, o_ref, buf_ref, sems):
    tt = COMBINE_TT
    tp = tt // COMBINE_PHASES

    def copy(j, k, p):
        return pltpu.make_async_copy(_row(ys_hbm, dest_ref[j * TOP_K + k]), _row(buf_ref, k * tt + j), sems.at[p])

    for p in range(COMBINE_PHASES):
        def issue(j, carry, p=p):
            for k in range(TOP_K):
                copy(j, k, p).start(priority=k % DMA_PRIORITIES)
            return carry
        lax.fori_loop(p * tp, (p + 1) * tp, issue, 0)

    for p in range(COMBINE_PHASES):
        def drain(j, carry, p=p):
            for k in range(TOP_K):
                copy(j, k, p).wait()
            return carry
        lax.fori_loop(p * tp, (p + 1) * tp, drain, 0)
        rows = pl.ds(p * tp, tp)
        meta = meta_ref[rows, :]
        wts = [jnp.broadcast_to(meta[:, 2 * TOP_K + k:2 * TOP_K + k + 1], (tp, LANES)) for k in range(TOP_K)]
        cols = []
        for c in range(ROW_TILES):
            acc = h_ref[rows, LANES * c:LANES * (c + 1)]
            for k in range(TOP_K):
                acc = acc + wts[k] * buf_ref[pl.ds((k * tt + p * tp) * ROW_TILES + c, tp, stride=ROW_TILES), :]
            cols.append(acc)
        o_ref[rows, :] = _rms(jnp.concatenate(cols, axis=1), fw_ref[...])


def _combine(dest, meta, h, final_w, ys):
    n = h.shape[0]
    tt = COMBINE_TT
    return pl.pallas_call(
        _combine_kernel,
        out_shape=jax.ShapeDtypeStruct((n, D_MODEL), F32),
        grid=(n // tt,),
        in_specs=[pl.BlockSpec((tt * TOP_K,), lambda i: (i,), memory_space=pltpu.SMEM),
                  pl.BlockSpec((tt, LANES), lambda i: (i, 0)),
                  pl.BlockSpec((tt, D_MODEL), lambda i: (i, 0)),
                  _full((1, D_MODEL)),
                  pl.BlockSpec(memory_space=pl.ANY)],
        out_specs=pl.BlockSpec((tt, D_MODEL), lambda i: (i, 0)),
        scratch_shapes=[pltpu.VMEM((TOP_K * tt * ROW_TILES, LANES), F32),
                        pltpu.SemaphoreType.DMA((COMBINE_PHASES,))],
        compiler_params=_params(("arbitrary",)),
        name="combine",
    )(dest, meta, h, final_w.reshape(1, -1), ys)


def _moe(h, norm_w, w_router, b_router, w_gu, b_gu, w_dn, b_dn, final_w):
    n = h.shape[0]
    t_rows, meta, cnt = _router(h, norm_w, w_router, b_router)
    eid = meta[:, :TOP_K].astype(I32)
    rank = meta[:, TOP_K:2 * TOP_K].astype(I32)
    counts = cnt[0, :N_EXPERTS].astype(I32)
    off = jnp.cumsum(counts) - counts
    hot = eid[..., None] == jnp.arange(N_EXPERTS, dtype=I32)
    dest = (jnp.sum(jnp.where(hot, off, 0), axis=-1) + rank).reshape(-1)
    xs = _dispatch(t_rows, dest)
    ys = _experts(xs, _expert_items(counts, n * TOP_K), w_gu, b_gu, w_dn, b_dn)
    return _combine(dest, meta, h, final_w, ys)


def kernel(x, mem, norm_mix_w, w_in, w_gate, b_gate, attn_sinks, conv_w, conv_b, dt_bias, a_log, d_skip,
           ssm_norm_w, w_attn_o, w_ssm_o, w_out, norm_cross_w, norm_mem_w, w_cq, w_ckv, w_co,
           norm_ffn_w, w_router, b_router, w_gu, b_gu, w_dn, b_dn, final_norm_w):
    bsz, seq, _ = x.shape
    assert norm_mix_w.shape[0] == 1, "single-layer block"
    x2 = x.reshape(bsz * seq, D_MODEL)
    q, kv, z, xbc, dt_raw, gates = _proj(x2, norm_mix_w[0], w_in[0], w_gate[0], b_gate[0])
    attn = _swa(q, kv, attn_sinks[0], bsz, seq)
    ssm = _ssd(xbc, z, dt_raw, conv_w[0], conv_b[0], dt_bias[0], a_log[0], d_skip[0], ssm_norm_w[0], bsz, seq)
    h = _outproj(x2, attn, ssm, gates, w_attn_o[0], w_ssm_o[0], w_out[0])
    mkv = _memkv(mem.reshape(bsz * MEM_TOKENS, D_MODEL), norm_mem_w[0], w_ckv[0], bsz)
    h = _cross(h, mkv, norm_cross_w[0], w_cq[0], w_co[0], seq)
    out = _moe(h, norm_ffn_w[0], w_router[0], b_router[0], w_gu[0], b_gu[0], w_dn[0], b_dn[0], final_norm_w)
    return out.reshape(bsz, seq, D_MODEL)
```

```python
import functools

import jax
import jax.numpy as jnp
from jax import lax
from jax.experimental import pallas as pl
from jax.experimental.pallas import tpu as pltpu

BF = jnp.bfloat16
F32 = jnp.float32
I32 = jnp.int32

D_MODEL = 1024
RMS_EPS = 1e-5
ATTN_HEADS = 16
ATTN_KV_HEADS = 2
HEAD_DIM = 64
WINDOW = 128
Q_DIM = ATTN_HEADS * HEAD_DIM
KV_DIM = ATTN_KV_HEADS * HEAD_DIM
D_INNER = 2 * D_MODEL
SSM_HEADS = 32
SSM_GROUPS = 4
D_STATE = 128
CONV_WIDTH = 4
CHUNK = 128
XBC_DIM = D_INNER + 2 * SSM_GROUPS * D_STATE
MEM_TOKENS = 256
CROSS_HEADS = 4
CROSS_HEAD_DIM = D_MODEL // CROSS_HEADS
N_EXPERTS = 32
TOP_K = 4
D_EXPERT = D_MODEL
SWIGLU_LIMIT = 7.0
SWIGLU_ALPHA = 1.702

LANES = 128
SUBLANES = 8
ROW_TILES = D_MODEL // LANES
NEG_BIG = -1e30
VMEM_LIMIT = 56 * 1024 * 1024

PROJ_TM = 512
DMA_PRIORITIES = 2
OUT_TM = 512
CROSS_TM = 512
ROUTER_TM = 512
DISPATCH_TT = 512
EXPERT_TM = 512
EXPERT_SUB = 256
COMBINE_TT = 512
COMBINE_PHASES = 4


def _rms(x, w):
    return x * lax.rsqrt(jnp.mean(x * x, axis=-1, keepdims=True) + RMS_EPS) * w


def _dot(a, b):
    return jnp.dot(a, b, preferred_element_type=F32)


def _dot_nt(a, b):
    return lax.dot_general(a, b, (((1,), (1,)), ((), ())), preferred_element_type=F32)


def _dot_f32(a, b):
    return jnp.dot(a, b, preferred_element_type=F32, precision=lax.Precision.HIGHEST)


def _params(sem):
    return pltpu.CompilerParams(dimension_semantics=sem, vmem_limit_bytes=VMEM_LIMIT)


def _full(shape):
    nd = len(shape)
    return pl.BlockSpec(shape, lambda *_: (0,) * nd, pipeline_mode=pl.Buffered(1))


def _proj_kernel(x_ref, nw_ref, wq_ref, wkv_ref, wz_ref, wxbc_ref, wdt_ref, wg_ref, bg_ref,
                 q_ref, kv_ref, z_ref, xbc_ref, dt_ref, g_ref):
    u = _rms(x_ref[...], nw_ref[...]).astype(BF)
    q_ref[...] = _dot(u, wq_ref[...]).astype(BF)
    kv_ref[...] = _dot(u, wkv_ref[...]).astype(BF)
    z_ref[...] = _dot(u, wz_ref[...]).astype(BF)
    xbc_ref[...] = _dot(u, wxbc_ref[...]).astype(BF)
    dt_ref[...] = _dot(u, wdt_ref[...])
    g_ref[...] = jax.nn.sigmoid(_dot(u, wg_ref[...]) + bg_ref[...]).astype(BF)


def _proj(x2, norm_w, w_in, w_gate, b_gate):
    n = x2.shape[0]
    tm = PROJ_TM
    s0, s1, s2, s3, s4 = Q_DIM, Q_DIM + KV_DIM, Q_DIM + 2 * KV_DIM, Q_DIM + 2 * KV_DIM + D_INNER, \
        Q_DIM + 2 * KV_DIM + D_INNER + XBC_DIM
    wq = (w_in[:, :s0] * (HEAD_DIM ** -0.5)).astype(BF)
    wk, wv = w_in[:, s0:s1], w_in[:, s1:s2]
    dup = lambda w: jnp.concatenate([w[:, :HEAD_DIM], w[:, :HEAD_DIM], w[:, HEAD_DIM:], w[:, HEAD_DIM:]], axis=1)
    wkv = jnp.concatenate([dup(wk), dup(wv)], axis=1).astype(BF)
    wz = w_in[:, s2:s3].astype(BF)
    wxbc = w_in[:, s3:s4].astype(BF)
    wdt = jnp.pad(w_in[:, s4:], ((0, 0), (0, LANES - SSM_HEADS))).astype(BF)
    wg = w_gate.astype(BF)
    outs = (
        jax.ShapeDtypeStruct((n, Q_DIM), BF),
        jax.ShapeDtypeStruct((n, 4 * LANES), BF),
        jax.ShapeDtypeStruct((n, D_INNER), BF),
        jax.ShapeDtypeStruct((n, XBC_DIM), BF),
        jax.ShapeDtypeStruct((n, LANES), F32),
        jax.ShapeDtypeStruct((n, 2 * D_MODEL), BF),
    )
    row = lambda w: pl.BlockSpec((tm, w), lambda i: (i, 0))
    return pl.pallas_call(
        _proj_kernel,
        out_shape=outs,
        grid=(n // tm,),
        in_specs=[row(D_MODEL), _full((1, D_MODEL)), _full(wq.shape), _full(wkv.shape), _full(wz.shape),
                  _full(wxbc.shape), _full(wdt.shape), _full(wg.shape), _full((1, 2 * D_MODEL))],
        out_specs=[row(Q_DIM), row(4 * LANES), row(D_INNER), row(XBC_DIM), row(LANES), row(2 * D_MODEL)],
        compiler_params=_params(("parallel",)),
        name="proj",
    )(x2, norm_w.reshape(1, -1), wq, wkv, wz, wxbc, wdt, wg, b_gate.reshape(1, -1))


PAIRS_PER_KV = ATTN_HEADS // ATTN_KV_HEADS // 2


SWA_BLOCKS = 4


def _swa_block(sink_ref, q_ref, kv_prev, kv_cur, bias, o_ref):
    w = WINDOW
    upper = lax.broadcasted_iota(I32, (2 * w, LANES), 1) >= HEAD_DIM
    zero = jnp.zeros((2 * w, LANES), BF)
    ones = jnp.ones((2 * w, LANES), BF)
    chains = [(kvh, r) for kvh in range(ATTN_KV_HEADS) for r in range(2)]
    half = lambda t, r: jnp.where(upper, t, zero) if r else jnp.where(upper, zero, t)
    scores, values, sinks = [], [], []
    for kvh, r in chains:
        ks = slice(LANES * kvh, LANES * (kvh + 1))
        vs = slice(LANES * (ATTN_KV_HEADS + kvh), LANES * (ATTN_KV_HEADS + kvh + 1))
        k2 = jnp.concatenate([kv_prev[:, ks], kv_cur[:, ks]], axis=0)
        v2 = jnp.concatenate([kv_prev[:, vs], kv_cur[:, vs]], axis=0)
        q4 = jnp.concatenate([q_ref[:, LANES * (kvh * PAIRS_PER_KV + p):LANES * (kvh * PAIRS_PER_KV + p + 1)]
                              for p in range(PAIRS_PER_KV)], axis=0)
        scores.append(_dot_nt(q4, half(k2, r)) + bias(kvh, r))
        values.append(jnp.concatenate([half(v2, r), ones], axis=1))
        sinks.append(jnp.concatenate(
            [jnp.full((w, 1), sink_ref[2 * (kvh * PAIRS_PER_KV + p) + r], F32) for p in range(PAIRS_PER_KV)],
            axis=0))
    maxes = [jnp.maximum(jnp.max(s, axis=-1, keepdims=True), sk) for s, sk in zip(scores, sinks)]
    probs = [jnp.exp(s - m).astype(BF) for s, m in zip(scores, maxes)]
    nds = [_dot(p, v) for p, v in zip(probs, values)]
    outs = [nd[:, :LANES] / (nd[:, LANES:] + jnp.exp(sk - m)) for nd, sk, m in zip(nds, sinks, maxes)]
    for kvh in range(ATTN_KV_HEADS):
        o4 = outs[2 * kvh] + outs[2 * kvh + 1]
        for p in range(PAIRS_PER_KV):
            pidx = kvh * PAIRS_PER_KV + p
            o_ref[:, LANES * pidx:LANES * (pidx + 1)] = o4[w * p:w * (p + 1)].astype(BF)


def _swa_kernel(sink_ref, q_ref, kvc_ref, kvp_ref, bias_ref, o_ref):
    has_prev = jnp.minimum(pl.program_id(1), 1)
    for sub in range(SWA_BLOCKS):
        rows = pl.ds(WINDOW * sub, WINDOW)
        kv_prev = kvp_ref if sub == 0 else kvc_ref.at[pl.ds(WINDOW * (sub - 1), WINDOW)]
        variant = has_prev if sub == 0 else 1
        _swa_block(sink_ref, q_ref.at[rows], kv_prev, kvc_ref.at[rows],
                   lambda kvh, r, v=variant: bias_ref[v, kvh, r], o_ref.at[rows])


def _swa_bias():
    qi = jnp.arange(WINDOW)[:, None]
    kj = jnp.arange(2 * WINDOW)[None, :]
    dist = qi + WINDOW - kj
    in_window = (dist >= 0) & (dist < WINDOW)
    heads = jnp.arange(ATTN_HEADS).reshape(ATTN_KV_HEADS, PAIRS_PER_KV, 2).transpose(0, 2, 1)
    slopes = jnp.exp2(-8.0 * (heads + 1).astype(F32) / ATTN_HEADS)
    bias = -slopes[..., None, None] * dist.astype(F32)
    tabs = []
    for has_prev in (False, True):
        mask = in_window & (has_prev | (kj >= WINDOW))
        tabs.append(jnp.where(mask, bias, NEG_BIG).reshape(ATTN_KV_HEADS, 2, PAIRS_PER_KV * WINDOW, 2 * WINDOW))
    return jnp.stack(tabs)


def _swa(q, kv, sinks, bsz, seq):
    nb = seq // (WINDOW * SWA_BLOCKS)
    n = bsz * seq
    rows = WINDOW * SWA_BLOCKS
    return pl.pallas_call(
        _swa_kernel,
        out_shape=jax.ShapeDtypeStruct((n, Q_DIM), BF),
        grid=(bsz, nb),
        in_specs=[
            pl.BlockSpec(memory_space=pltpu.SMEM),
            pl.BlockSpec((rows, Q_DIM), lambda b, j: (b * nb + j, 0)),
            pl.BlockSpec((rows, 4 * LANES), lambda b, j: (b * nb + j, 0)),
            pl.BlockSpec((WINDOW, 4 * LANES), lambda b, j: (jnp.maximum((b * nb + j) * SWA_BLOCKS - 1, 0), 0)),
            _full((2, ATTN_KV_HEADS, 2, PAIRS_PER_KV * WINDOW, 2 * WINDOW)),
        ],
        out_specs=pl.BlockSpec((rows, Q_DIM), lambda b, j: (b * nb + j, 0)),
        compiler_params=_params(("parallel", "parallel")),
        name="swa",
    )(sinks.astype(F32), q, kv, kv, _swa_bias())


def _silu(x):
    hx = 0.5 * x
    return hx + hx * jnp.tanh(hx)


def _softplus(x):
    return jnp.maximum(x, 0.0) + jnp.log1p(jnp.exp(-jnp.abs(x)))


SSD_CHUNKS = 2


def _ssd_chunk(xbc_ref, z_ref, dtc_ref, dt_row_raw, cw_ref, cb_ref, dtbc_ref, dtbr_ref, alc_ref, alr_ref,
               dskip_ref, nw_ref, o_ref, state_ref, tail_ref):
    L = CHUNK
    x_raw = xbc_ref[...].astype(F32)
    prev8 = tail_ref[...]
    row8 = lax.broadcasted_iota(I32, (SUBLANES, XBC_DIM), 0)
    acc = x_raw * cw_ref[CONV_WIDTH - 1:CONV_WIDTH, :] + cb_ref[...]
    for s in range(1, CONV_WIDTH):
        sh = pltpu.roll(x_raw, s, axis=0)
        top = jnp.where(row8 < s, pltpu.roll(prev8, s, axis=0), sh[:SUBLANES])
        sh = jnp.concatenate([top, sh[SUBLANES:]], axis=0)
        acc = acc + sh * cw_ref[CONV_WIDTH - 1 - s:CONV_WIDTH - s, :]
    tail_ref[...] = x_raw[L - SUBLANES:, :]
    xc = _silu(acc)

    dt_col = _softplus(dtc_ref[...] + dtbc_ref[...])
    dt_row = _softplus(dt_row_raw + dtbr_ref[...])
    a_col = dt_col * (-jnp.exp(alc_ref[...]))
    a_row = dt_row * (-jnp.exp(alr_ref[...]))
    ii = lax.broadcasted_iota(I32, (L, L), 0)
    jj = lax.broadcasted_iota(I32, (L, L), 1)
    causal = ii >= jj
    tri_l = jnp.where(causal, 1.0, 0.0).astype(F32)
    tri_u = jnp.where(ii <= jj, 1.0, 0.0).astype(F32)
    acum_col = _dot_f32(tri_l, a_col)
    acum_row = _dot_f32(a_row, tri_u)
    decay_col = jnp.exp(acum_col)
    last_col = decay_col[L - 1:L, :]
    last_row = acum_row[:, L - 1:L]
    wrow = jnp.exp(last_row - acum_row) * dt_row
    shifted_row = acum_row - jnp.log(dt_row)

    upper = lax.broadcasted_iota(I32, (L, LANES), 1) >= 64
    upper1 = upper[:1]
    lower = jnp.logical_not(upper)
    pairs_per_group = SSM_HEADS // SSM_GROUPS // 2
    group_w = D_INNER // SSM_GROUPS
    for g in range(SSM_GROUPS):
        b_g = xc[:, D_INNER + D_STATE * g:D_INNER + D_STATE * (g + 1)]
        c_g = xc[:, D_INNER + D_STATE * (SSM_GROUPS + g):D_INNER + D_STATE * (SSM_GROUPS + g + 1)]
        c_bf = c_g.astype(BF)
        cb = _dot_nt(c_bf, b_g.astype(BF))
        b_gt = b_g.astype(F32).T
        pairs = range(g * pairs_per_group, (g + 1) * pairs_per_group)
        states = [state_ref[p] for p in pairs]
        y_off = _dot(c_bf, jnp.concatenate(states, axis=1).astype(BF))
        ys = []
        for j, pidx in enumerate(pairs):
            x_pair = xc[:, LANES * pidx:LANES * (pidx + 1)]
            xms = (jnp.where(lower, x_pair, 0.0).astype(BF), jnp.where(upper, x_pair, 0.0).astype(BF))
            acc_y = None
            acc_s = None
            for r in range(2):
                h = 2 * pidx + r
                acol = jnp.broadcast_to(acum_col[:, h:h + 1], (L, L))
                m_h = cb * jnp.exp(jnp.where(causal, acol - shifted_row[h:h + 1, :], NEG_BIG))
                y = _dot(m_h.astype(BF), xms[r])
                acc_y = y if acc_y is None else acc_y + y
                bw = (b_gt * wrow[h:h + 1, :]).astype(BF)
                sn = _dot(bw, xms[r])
                acc_s = sn if acc_s is None else acc_s + sn
            h0, h1 = 2 * pidx, 2 * pidx + 1
            dec = jnp.where(upper, jnp.broadcast_to(decay_col[:, h1:h1 + 1], (L, LANES)),
                            jnp.broadcast_to(decay_col[:, h0:h0 + 1], (L, LANES)))
            d0 = jnp.broadcast_to(last_col[:, h0:h0 + 1], (1, LANES))
            d1 = jnp.broadcast_to(last_col[:, h1:h1 + 1], (1, LANES))
            state_ref[pidx] = states[j] * jnp.where(upper1, d1, d0) + acc_s
            ys.append(acc_y + y_off[:, LANES * j:LANES * (j + 1)] * dec
                      + x_pair * dskip_ref[:, LANES * pidx:LANES * (pidx + 1)])
        y_g = jnp.concatenate(ys, axis=1)
        z_g = z_ref[:, group_w * g:group_w * (g + 1)].astype(F32)
        y_g = y_g * _silu(z_g)
        y_g = _rms(y_g, nw_ref[:, group_w * g:group_w * (g + 1)])
        o_ref[:, group_w * g:group_w * (g + 1)] = y_g.astype(BF)


def _ssd_kernel(xbc_ref, z_ref, dtc_ref, dtr_ref, cw_ref, cb_ref, dtbc_ref, dtbr_ref, alc_ref, alr_ref,
                dskip_ref, nw_ref, o_ref, state_ref, tail_ref):
    @pl.when(pl.program_id(1) == 0)
    def _():
        state_ref[...] = jnp.zeros_like(state_ref)
        tail_ref[...] = jnp.zeros_like(tail_ref)

    for sub in range(SSD_CHUNKS):
        rows = pl.ds(CHUNK * sub, CHUNK)
        _ssd_chunk(xbc_ref.at[rows], z_ref.at[rows], dtc_ref.at[rows], dtr_ref[:, CHUNK * sub:CHUNK * (sub + 1)],
                   cw_ref, cb_ref, dtbc_ref, dtbr_ref, alc_ref, alr_ref, dskip_ref, nw_ref, o_ref.at[rows],
                   state_ref, tail_ref)


def _ssd(xbc, z, dt_raw, conv_w, conv_b, dt_bias, a_log, d_skip, norm_w, bsz, seq):
    rows = CHUNK * SSD_CHUNKS
    nc = seq // rows
    n = bsz * seq
    pad = LANES - SSM_HEADS
    dt_row = dt_raw[:, :SSM_HEADS].T
    dtb_col = jnp.pad(dt_bias, (0, pad)).reshape(1, LANES)
    al_col = jnp.pad(a_log, (0, pad)).reshape(1, LANES)
    dtb_row = jnp.broadcast_to(dt_bias[:, None], (SSM_HEADS, CHUNK))
    al_row = jnp.broadcast_to(a_log[:, None], (SSM_HEADS, CHUNK))
    dskip = jnp.repeat(d_skip, D_INNER // SSM_HEADS).reshape(1, D_INNER)
    blk = lambda w: pl.BlockSpec((rows, w), lambda b, j: (b * nc + j, 0))
    return pl.pallas_call(
        _ssd_kernel,
        out_shape=jax.ShapeDtypeStruct((n, D_INNER), BF),
        grid=(bsz, nc),
        in_specs=[
            blk(XBC_DIM), blk(D_INNER), blk(LANES),
            pl.BlockSpec((SSM_HEADS, rows), lambda b, j: (0, b * nc + j)),
            _full((CONV_WIDTH, XBC_DIM)), _full((1, XBC_DIM)),
            _full((1, LANES)), _full((SSM_HEADS, CHUNK)), _full((1, LANES)), _full((SSM_HEADS, CHUNK)),
            _full((1, D_INNER)), _full((1, D_INNER)),
        ],
        out_specs=blk(D_INNER),
        scratch_shapes=[pltpu.VMEM((SSM_HEADS // 2, D_STATE, LANES), F32),
                        pltpu.VMEM((SUBLANES, XBC_DIM), F32)],
        compiler_params=_params(("parallel", "arbitrary")),
        name="ssd",
    )(xbc, z, dt_raw, dt_row, conv_w, conv_b.reshape(1, -1), dtb_col, dtb_row, al_col, al_row,
      dskip, norm_w.reshape(1, -1))


def _outproj_kernel(x_ref, attn_ref, ssm_ref, g_ref, wa_ref, ws_ref, wo_ref, h_ref):
    a = _dot(attn_ref[...], wa_ref[...])
    s = _dot(ssm_ref[...], ws_ref[...])
    merged = g_ref[:, :D_MODEL].astype(F32) * a + g_ref[:, D_MODEL:].astype(F32) * s
    h_ref[...] = x_ref[...] + _dot(merged.astype(BF), wo_ref[...])


def _outproj(x2, attn, ssm, gates, w_attn_o, w_ssm_o, w_out):
    n = x2.shape[0]
    tm = OUT_TM
    row = lambda w: pl.BlockSpec((tm, w), lambda i: (i, 0))
    return pl.pallas_call(
        _outproj_kernel,
        out_shape=jax.ShapeDtypeStruct((n, D_MODEL), F32),
        grid=(n // tm,),
        in_specs=[row(D_MODEL), row(Q_DIM), row(D_INNER), row(2 * D_MODEL),
                  _full((Q_DIM, D_MODEL)), _full((D_INNER, D_MODEL)), _full((D_MODEL, D_MODEL))],
        out_specs=row(D_MODEL),
        compiler_params=_params(("parallel",)),
        name="outproj",
    )(x2, attn, ssm, gates, w_attn_o.astype(BF), w_ssm_o.astype(BF), w_out.astype(BF))


def _memkv_kernel(mem_ref, nw_ref, w_ref, kv_ref):
    mn = _rms(mem_ref[...], nw_ref[...]).astype(BF)
    kv_ref[...] = _dot(mn, w_ref[...]).astype(BF)


def _memkv(mem2, norm_w, w_ckv, bsz):
    return pl.pallas_call(
        _memkv_kernel,
        out_shape=jax.ShapeDtypeStruct((bsz * MEM_TOKENS, 2 * D_MODEL), BF),
        grid=(bsz,),
        in_specs=[pl.BlockSpec((MEM_TOKENS, D_MODEL), lambda b: (b, 0)), _full((1, D_MODEL)),
                  _full((D_MODEL, 2 * D_MODEL))],
        out_specs=pl.BlockSpec((MEM_TOKENS, 2 * D_MODEL), lambda b: (b, 0)),
        compiler_params=_params(("parallel",)),
        name="memkv",
    )(mem2, norm_w.reshape(1, -1), w_ckv.astype(BF))


def _cross_kernel(h_ref, kv_ref, nw_ref, wq_ref, wo_ref, o_ref):
    h = h_ref[...]
    hn = _rms(h, nw_ref[...]).astype(BF)
    q = (_dot(hn, wq_ref[...]) * (CROSS_HEAD_DIM ** -0.5)).astype(BF)
    outs = []
    for hd in range(CROSS_HEADS):
        lo, hi = CROSS_HEAD_DIM * hd, CROSS_HEAD_DIM * (hd + 1)
        s = _dot_nt(q[:, lo:hi], kv_ref[:, lo:hi])
        p = jnp.exp(s - jnp.max(s, axis=-1, keepdims=True))
        den = jnp.sum(p, axis=-1, keepdims=True)
        outs.append((_dot(p.astype(BF), kv_ref[:, D_MODEL + lo:D_MODEL + hi]) / den).astype(BF))
    o = jnp.concatenate(outs, axis=1)
    o_ref[...] = h + _dot(o, wo_ref[...])


def _cross(h, kv, norm_w, w_cq, w_co, seq):
    n = h.shape[0]
    tm = CROSS_TM
    per_b = seq // tm
    row = pl.BlockSpec((tm, D_MODEL), lambda i: (i, 0))
    return pl.pallas_call(
        _cross_kernel,
        out_shape=jax.ShapeDtypeStruct((n, D_MODEL), F32),
        grid=(n // tm,),
        in_specs=[row, pl.BlockSpec((MEM_TOKENS, 2 * D_MODEL), lambda i: (i // per_b, 0)),
                  _full((1, D_MODEL)), _full((D_MODEL, D_MODEL)), _full((D_MODEL, D_MODEL))],
        out_specs=row,
        compiler_params=_params(("parallel",)),
        name="cross",
    )(h, kv, norm_w.reshape(1, -1), w_cq.astype(BF), w_co.astype(BF))


def _router_kernel(h_ref, nw_ref, wr_ref, br_ref, t_ref, meta_ref, cnt_ref, carry_ref):
    i = pl.program_id(0)
    tm = ROUTER_TM

    @pl.when(i == 0)
    def _():
        carry_ref[...] = jnp.zeros_like(carry_ref)

    t = _rms(h_ref[...], nw_ref[...])
    for c in range(ROW_TILES):
        t_ref[pl.ds(c, tm, stride=ROW_TILES), :] = t[:, LANES * c:LANES * (c + 1)]
    lane = lax.broadcasted_iota(I32, (tm, LANES), 1)
    t_hi = t.astype(BF)
    t_lo = (t - t_hi.astype(F32)).astype(BF)
    w_hi = wr_ref[...].astype(BF)
    w_lo = (wr_ref[...] - w_hi.astype(F32)).astype(BF)
    raw = _dot(t_hi, w_hi) + (_dot(t_lo, w_hi) + _dot(t_hi, w_lo))
    logits = jnp.where(lane < N_EXPERTS, raw + br_ref[...], NEG_BIG)
    work = logits
    vals, ids, hots = [], [], []
    for _k in range(TOP_K):
        m = jnp.max(work, axis=-1, keepdims=True)
        idx = jnp.min(jnp.where(work == m, lane, LANES), axis=-1, keepdims=True)
        hot = lane == idx
        vals.append(m)
        ids.append(idx)
        hots.append(hot)
        work = jnp.where(hot, NEG_BIG * 2.0, work)
    es = [jnp.exp(v - vals[0]) for v in vals]
    den = es[0] + es[1] + es[2] + es[3]
    member = jnp.zeros((tm, LANES), F32)
    for hot in hots:
        member = member + jnp.where(hot, 1.0, 0.0)
    ii = lax.broadcasted_iota(I32, (tm, tm), 0)
    jj = lax.broadcasted_iota(I32, (tm, tm), 1)
    strict = jnp.where(ii > jj, 1.0, 0.0).astype(BF)
    carry = carry_ref[0:1, :]
    before = _dot(strict, member.astype(BF)) + carry
    meta = jnp.zeros((tm, LANES), F32)
    for k in range(TOP_K):
        rank = jnp.sum(jnp.where(hots[k], before, 0.0), axis=-1, keepdims=True)
        meta = jnp.where(lane == k, ids[k].astype(F32), meta)
        meta = jnp.where(lane == TOP_K + k, rank, meta)
        meta = jnp.where(lane == 2 * TOP_K + k, es[k] / den, meta)
    meta_ref[...] = meta
    new_carry = carry + jnp.sum(member, axis=0, keepdims=True)
    carry_ref[...] = jnp.broadcast_to(new_carry, carry_ref.shape)
    cnt_ref[...] = jnp.broadcast_to(new_carry, cnt_ref.shape)


def _router(h, norm_w, w_router, b_router):
    n = h.shape[0]
    tm = ROUTER_TM
    wr = jnp.pad(w_router, ((0, 0), (0, LANES - N_EXPERTS)))
    br = jnp.pad(b_router, (0, LANES - N_EXPERTS)).reshape(1, LANES)
    return pl.pallas_call(
        _router_kernel,
        out_shape=(jax.ShapeDtypeStruct((n * ROW_TILES, LANES), F32),
                   jax.ShapeDtypeStruct((n, LANES), F32),
                   jax.ShapeDtypeStruct((SUBLANES, LANES), F32)),
        grid=(n // tm,),
        in_specs=[pl.BlockSpec((tm, D_MODEL), lambda i: (i, 0)), _full((1, D_MODEL)),
                  _full((D_MODEL, LANES)), _full((1, LANES))],
        out_specs=[pl.BlockSpec((tm * ROW_TILES, LANES), lambda i: (i, 0)),
                   pl.BlockSpec((tm, LANES), lambda i: (i, 0)),
                   pl.BlockSpec((SUBLANES, LANES), lambda i: (0, 0))],
        scratch_shapes=[pltpu.VMEM((SUBLANES, LANES), F32)],
        compiler_params=_params(("arbitrary",)),
        name="router",
    )(h, norm_w.reshape(1, -1), wr, br)


def _row(ref, r):
    return ref.at[pl.ds(pl.multiple_of(r * ROW_TILES, ROW_TILES), ROW_TILES)]


def _dispatch_kernel(dest_ref, t_ref, xs_hbm, sem):
    tt = DISPATCH_TT

    def copy(j, k):
        return pltpu.make_async_copy(_row(t_ref, j), _row(xs_hbm, dest_ref[j * TOP_K + k]), sem)

    def issue(j, carry):
        for k in range(TOP_K):
            copy(j, k).start(priority=k % DMA_PRIORITIES)
        return carry

    def drain(j, carry):
        for k in range(TOP_K):
            copy(j, k).wait()
        return carry

    lax.fori_loop(0, tt, issue, 0)
    lax.fori_loop(0, tt, drain, 0)


def _dispatch(t_rows, dest):
    n = t_rows.shape[0] // ROW_TILES
    tt = DISPATCH_TT
    return pl.pallas_call(
        _dispatch_kernel,
        out_shape=jax.ShapeDtypeStruct((n * TOP_K * ROW_TILES, LANES), F32),
        grid=(n // tt,),
        in_specs=[pl.BlockSpec((tt * TOP_K,), lambda i: (i,), memory_space=pltpu.SMEM),
                  pl.BlockSpec((tt * ROW_TILES, LANES), lambda i: (i, 0))],
        out_specs=pl.BlockSpec(memory_space=pl.ANY),
        scratch_shapes=[pltpu.SemaphoreType.DMA],
        compiler_params=_params(("arbitrary",)),
        name="dispatch",
    )(dest, t_rows)


def _expert_kernel(tile_ref, exp_ref, lo_ref, hi_ref, first_ref, newexp_ref, slot_ref, next_ref,
                   xs_ref, wgu_hbm, bgu_ref, wdn_hbm, bdn_ref, ys_ref, wgu_bf, wdn_bf, wgu_f32, wdn_f32, sems):
    w = pl.program_id(0)
    tm = EXPERT_TM
    lo = lo_ref[w]
    hi = hi_ref[w]

    def weight_copies(e, slot):
        return (pltpu.make_async_copy(wgu_hbm.at[e], wgu_f32.at[slot], sems.at[0, slot]),
                pltpu.make_async_copy(wdn_hbm.at[e], wdn_f32.at[slot], sems.at[1, slot]))

    @pl.when(w == 0)
    def _():
        for cp in weight_copies(exp_ref[0], 0):
            cp.start()

    @pl.when(newexp_ref[w] == 1)
    def _():
        slot = slot_ref[w]
        for cp in weight_copies(exp_ref[w], slot):
            cp.wait()
        wgu_bf[...] = wgu_f32[slot].astype(BF)
        wdn_bf[...] = wdn_f32[slot].astype(BF)

        @pl.when(next_ref[w] >= 0)
        def _():
            for cp in weight_copies(next_ref[w], 1 - slot):
                cp.start()

    sb = EXPERT_SUB
    for part in range(tm // sb):
        start = tile_ref[w] * tm + part * sb

        @pl.when(jnp.minimum(hi, start + sb) > jnp.maximum(lo, start))
        def _(part=part, start=start):
            chunk = lambda c: pl.ds(part * sb * ROW_TILES + c, sb, stride=ROW_TILES)
            x = jnp.concatenate([xs_ref[chunk(c), :] for c in range(ROW_TILES)], axis=1).astype(BF)
            gu = _dot(x, wgu_bf[...]) + bgu_ref[0]
            g = jnp.minimum(gu[:, :D_EXPERT], SWIGLU_LIMIT)
            up = jnp.clip(gu[:, D_EXPERT:], -SWIGLU_LIMIT, SWIGLU_LIMIT)
            hg = 0.5 * g
            act = (up + 1.0) * (hg + hg * jnp.tanh(SWIGLU_ALPHA * hg))
            y = _dot(act.astype(BF), wdn_bf[...]) + bdn_ref[0]
            rows = start + lax.broadcasted_iota(I32, (sb, LANES), 0)
            mine = (rows >= lo) & (rows < hi)
            is_first = lo <= start

            @pl.when(is_first)
            def _():
                for c in range(ROW_TILES):
                    ys_ref[chunk(c), :] = jnp.where(mine, y[:, LANES * c:LANES * (c + 1)], 0.0)

            @pl.when(jnp.logical_not(is_first))
            def _():
                for c in range(ROW_TILES):
                    ys_ref[chunk(c), :] = jnp.where(mine, y[:, LANES * c:LANES * (c + 1)], ys_ref[chunk(c), :])


def _experts(xs, items, w_gu, b_gu, w_dn, b_dn):
    n_items = items[0].shape[0]
    tm = EXPERT_TM
    grid_spec = pltpu.PrefetchScalarGridSpec(
        num_scalar_prefetch=len(items),
        grid=(n_items,),
        in_specs=[
            pl.BlockSpec((tm * ROW_TILES, LANES), lambda w, t, e, *_: (t[w], 0)),
            pl.BlockSpec(memory_space=pl.ANY),
            pl.BlockSpec((1, 1, 2 * D_EXPERT), lambda w, t, e, *_: (e[w], 0, 0)),
            pl.BlockSpec(memory_space=pl.ANY),
            pl.BlockSpec((1, 1, D_MODEL), lambda w, t, e, *_: (e[w], 0, 0)),
        ],
        out_specs=pl.BlockSpec((tm * ROW_TILES, LANES), lambda w, t, e, *_: (t[w], 0)),
        scratch_shapes=[pltpu.VMEM((D_MODEL, 2 * D_EXPERT), BF), pltpu.VMEM((D_EXPERT, D_MODEL), BF),
                        pltpu.VMEM((2, D_MODEL, 2 * D_EXPERT), F32), pltpu.VMEM((2, D_EXPERT, D_MODEL), F32),
                        pltpu.SemaphoreType.DMA((2, 2))],
    )
    return pl.pallas_call(
        _expert_kernel,
        out_shape=jax.ShapeDtypeStruct(xs.shape, F32),
        grid_spec=grid_spec,
        compiler_params=_params(("arbitrary",)),
        name="experts",
    )(*items, xs, w_gu, b_gu.reshape(N_EXPERTS, 1, -1), w_dn, b_dn.reshape(N_EXPERTS, 1, -1))


def _expert_items(counts, n_rows):
    tm = EXPERT_TM
    n_tiles = n_rows // tm
    n_items = n_tiles + N_EXPERTS - 1
    end = jnp.cumsum(counts)
    off = end - counts
    first_tile = off // tm
    last_tile = jnp.maximum(end - 1, 0) // tm
    per = jnp.where(counts > 0, last_tile - first_tile + 1, 0)
    cum = jnp.cumsum(per)
    start = cum - per
    total = cum[-1]
    w = jnp.arange(n_items, dtype=I32)
    wc = jnp.minimum(w, total - 1)
    e_of = jnp.minimum(jnp.sum((cum[None, :] <= wc[:, None]).astype(I32), axis=1), N_EXPERTS - 1)
    sel = e_of[:, None] == jnp.arange(N_EXPERTS, dtype=I32)
    pick = lambda v: jnp.sum(jnp.where(sel, v[None, :], 0), axis=1)
    t_of = (pick(first_tile) + (wc - pick(start))).astype(I32)
    valid = w < total
    lo = jnp.where(valid, jnp.maximum(pick(off), t_of * tm), 0).astype(I32)
    hi = jnp.where(valid, jnp.minimum(pick(end), (t_of + 1) * tm), 0).astype(I32)
    prev_t = jnp.concatenate([jnp.full((1,), -1, I32), t_of[:-1]])
    first = (t_of != prev_t).astype(I32)
    prev_e = jnp.concatenate([jnp.full((1,), -1, I32), e_of[:-1]])
    newexp = (e_of != prev_e).astype(I32)
    slot = (jnp.cumsum(newexp) - 1) % 2
    later_change = (w[None, :] > w[:, None]) & (newexp[None, :] == 1)
    nxt = jnp.min(jnp.where(later_change, w[None, :], n_items), axis=1)
    next_e = jnp.sum(jnp.where(nxt[:, None] == w[None, :], e_of[None, :], 0), axis=1)
    next_e = jnp.where(nxt < n_items, next_e, -1).astype(I32)
    return t_of, e_of, lo, hi, first, newexp, slot.astype(I32), next_e


def _combine_kernel(dest_ref, meta_ref, h_ref, fw_ref, ys_hbm, o_ref, buf_ref, sems):
    tt = COMBINE_TT
    tp = tt // COMBINE_PHASES

    def copy(j, k, p):
        return pltpu.make_async_copy(_row(ys_hbm, dest_ref[j * TOP_K + k]), _row(buf_ref, k * tt + j), sems.at[p])

    for p in range(COMBINE_PHASES):
        def issue(j, carry, p=p):
            for k in range(TOP_K):
                copy(j, k, p).start(priority=k % DMA_PRIORITIES)
            return carry
        lax.fori_loop(p * tp, (p + 1) * tp, issue, 0)

    for p in range(COMBINE_PHASES):
        def drain(j, carry, p=p):
            for k in range(TOP_K):
                copy(j, k, p).wait()
            return carry
        lax.fori_loop(p * tp, (p + 1) * tp, drain, 0)
        rows = pl.ds(p * tp, tp)
        meta = meta_ref[rows, :]
        wts = [jnp.broadcast_to(meta[:, 2 * TOP_K + k:2 * TOP_K + k + 1], (tp, LANES)) for k in range(TOP_K)]
        cols = []
        for c in range(ROW_TILES):
            acc = h_ref[rows, LANES * c:LANES * (c + 1)]
            for k in range(TOP_K):
                acc = acc + wts[k] * buf_ref[pl.ds((k * tt + p * tp) * ROW_TILES + c, tp, stride=ROW_TILES), :]
            cols.append(acc)
        o_ref[rows, :] = _rms(jnp.concatenate(cols, axis=1), fw_ref[...])


def _combine(dest, meta, h, final_w, ys):
    n = h.shape[0]
    tt = COMBINE_TT
    return pl.pallas_call(
        _combine_kernel,
        out_shape=jax.ShapeDtypeStruct((n, D_MODEL), F32),
        grid=(n // tt,),
        in_specs=[pl.BlockSpec((tt * TOP_K,), lambda i: (i,), memory_space=pltpu.SMEM),
                  pl.BlockSpec((tt, LANES), lambda i: (i, 0)),
                  pl.BlockSpec((tt, D_MODEL), lambda i: (i, 0)),
                  _full((1, D_MODEL)),
                  pl.BlockSpec(memory_space=pl.ANY)],
        out_specs=pl.BlockSpec((tt, D_MODEL), lambda i: (i, 0)),
        scratch_shapes=[pltpu.VMEM((TOP_K * tt * ROW_TILES, LANES), F32),
                        pltpu.SemaphoreType.DMA((COMBINE_PHASES,))],
        compiler_params=_params(("arbitrary",)),
        name="combine",
    )(dest, meta, h, final_w.reshape(1, -1), ys)


def _moe(h, norm_w, w_router, b_router, w_gu, b_gu, w_dn, b_dn, final_w):
    n = h.shape[0]
    t_rows, meta, cnt = _router(h, norm_w, w_router, b_router)
    eid = meta[:, :TOP_K].astype(I32)
    rank = meta[:, TOP_K:2 * TOP_K].astype(I32)
    counts = cnt[0, :N_EXPERTS].astype(I32)
    off = jnp.cumsum(counts) - counts
    hot = eid[..., None] == jnp.arange(N_EXPERTS, dtype=I32)
    dest = (jnp.sum(jnp.where(hot, off, 0), axis=-1) + rank).reshape(-1)
    xs = _dispatch(t_rows, dest)
    ys = _experts(xs, _expert_items(counts, n * TOP_K), w_gu, b_gu, w_dn, b_dn)
    return _combine(dest, meta, h, final_w, ys)


def kernel(x, mem, norm_mix_w, w_in, w_gate, b_gate, attn_sinks, conv_w, conv_b, dt_bias, a_log, d_skip,
           ssm_norm_w, w_attn_o, w_ssm_o, w_out, norm_cross_w, norm_mem_w, w_cq, w_ckv, w_co,
           norm_ffn_w, w_router, b_router, w_gu, b_gu, w_dn, b_dn, final_norm_w):
    bsz, seq, _ = x.shape
    assert norm_mix_w.shape[0] == 1, "single-layer block"
    x2 = x.reshape(bsz * seq, D_MODEL)
    q, kv, z, xbc, dt_raw, gates = _proj(x2, norm_mix_w[0], w_in[0], w_gate[0], b_gate[0])
    attn = _swa(q, kv, attn_sinks[0], bsz, seq)
    ssm = _ssd(xbc, z, dt_raw, conv_w[0], conv_b[0], dt_bias[0], a_log[0], d_skip[0], ssm_norm_w[0], bsz, seq)
    h = _outproj(x2, attn, ssm, gates, w_attn_o[0], w_ssm_o[0], w_out[0])
    mkv = _memkv(mem.reshape(bsz * MEM_TOKENS, D_MODEL), norm_mem_w[0], w_ckv[0], bsz)
    h = _cross(h, mkv, norm_cross_w[0], w_cq[0], w_co[0], seq)
    out = _moe(h, norm_ffn_w[0], w_router[0], b_router[0], w_gu[0], b_gu[0], w_dn[0], b_dn[0], final_norm_w)
    return out.reshape(bsz, seq, D_MODEL)
```

```python
import functools

import jax
import jax.numpy as jnp
from jax import lax
from jax.experimental import pallas as pl
from jax.experimental.pallas import tpu as pltpu

BF = jnp.bfloat16
F32 = jnp.float32
I32 = jnp.int32

D_MODEL = 1024
RMS_EPS = 1e-5
ATTN_HEADS = 16
ATTN_KV_HEADS = 2
HEAD_DIM = 64
WINDOW = 128
Q_DIM = ATTN_HEADS * HEAD_DIM
KV_DIM = ATTN_KV_HEADS * HEAD_DIM
D_INNER = 2 * D_MODEL
SSM_HEADS = 32
SSM_GROUPS = 4
D_STATE = 128
CONV_WIDTH = 4
CHUNK = 128
XBC_DIM = D_INNER + 2 * SSM_GROUPS * D_STATE
MEM_TOKENS = 256
CROSS_HEADS = 4
CROSS_HEAD_DIM = D_MODEL // CROSS_HEADS
N_EXPERTS = 32
TOP_K = 4
D_EXPERT = D_MODEL
SWIGLU_LIMIT = 7.0
SWIGLU_ALPHA = 1.702

LANES = 128
SUBLANES = 8
ROW_TILES = D_MODEL // LANES
NEG_BIG = -1e30
VMEM_LIMIT = 56 * 1024 * 1024

PROJ_TM = 512
DMA_PRIORITIES = 2
OUT_TM = 512
CROSS_TM = 512
ROUTER_TM = 512
DISPATCH_TT = 512
EXPERT_TM = 512
EXPERT_SUB = 256
COMBINE_TT = 512
COMBINE_PHASES = 4


def _rms(x, w):
    return x * lax.rsqrt(jnp.mean(x * x, axis=-1, keepdims=True) + RMS_EPS) * w


def _dot(a, b):
    return jnp.dot(a, b, preferred_element_type=F32)


def _dot_nt(a, b):
    return lax.dot_general(a, b, (((1,), (1,)), ((), ())), preferred_element_type=F32)


def _dot_f32(a, b):
    return jnp.dot(a, b, preferred_element_type=F32, precision=lax.Precision.HIGHEST)


def _params(sem):
    return pltpu.CompilerParams(dimension_semantics=sem, vmem_limit_bytes=VMEM_LIMIT)


def _full(shape):
    nd = len(shape)
    return pl.BlockSpec(shape, lambda *_: (0,) * nd, pipeline_mode=pl.Buffered(1))


def _proj_kernel(x_ref, nw_ref, wq_ref, wkv_ref, wz_ref, wxbc_ref, wdt_ref, wg_ref, bg_ref,
                 q_ref, kv_ref, z_ref, xbc_ref, dt_ref, g_ref):
    u = _rms(x_ref[...], nw_ref[...]).astype(BF)
    q_ref[...] = _dot(u, wq_ref[...]).astype(BF)
    kv_ref[...] = _dot(u, wkv_ref[...]).astype(BF)
    z_ref[...] = _dot(u, wz_ref[...]).astype(BF)
    xbc_ref[...] = _dot(u, wxbc_ref[...]).astype(BF)
    dt_ref[...] = _dot(u, wdt_ref[...])
    g_ref[...] = jax.nn.sigmoid(_dot(u, wg_ref[...]) + bg_ref[...]).astype(BF)


def _proj(x2, norm_w, w_in, w_gate, b_gate):
    n = x2.shape[0]
    tm = PROJ_TM
    s0, s1, s2, s3, s4 = Q_DIM, Q_DIM + KV_DIM, Q_DIM + 2 * KV_DIM, Q_DIM + 2 * KV_DIM + D_INNER, \
        Q_DIM + 2 * KV_DIM + D_INNER + XBC_DIM
    wq = (w_in[:, :s0] * (HEAD_DIM ** -0.5)).astype(BF)
    wk, wv = w_in[:, s0:s1], w_in[:, s1:s2]
    dup = lambda w: jnp.concatenate([w[:, :HEAD_DIM], w[:, :HEAD_DIM], w[:, HEAD_DIM:], w[:, HEAD_DIM:]], axis=1)
    wkv = jnp.concatenate([dup(wk), dup(wv)], axis=1).astype(BF)
    wz = w_in[:, s2:s3].astype(BF)
    wxbc = w_in[:, s3:s4].astype(BF)
    wdt = jnp.pad(w_in[:, s4:], ((0, 0), (0, LANES - SSM_HEADS))).astype(BF)
    wg = w_gate.astype(BF)
    outs = (
        jax.ShapeDtypeStruct((n, Q_DIM), BF),
        jax.ShapeDtypeStruct((n, 4 * LANES), BF),
        jax.ShapeDtypeStruct((n, D_INNER), BF),
        jax.ShapeDtypeStruct((n, XBC_DIM), BF),
        jax.ShapeDtypeStruct((n, LANES), F32),
        jax.ShapeDtypeStruct((n, 2 * D_MODEL), BF),
    )
    row = lambda w: pl.BlockSpec((tm, w), lambda i: (i, 0))
    return pl.pallas_call(
        _proj_kernel,
        out_shape=outs,
        grid=(n // tm,),
        in_specs=[row(D_MODEL), _full((1, D_MODEL)), _full(wq.shape), _full(wkv.shape), _full(wz.shape),
                  _full(wxbc.shape), _full(wdt.shape), _full(wg.shape), _full((1, 2 * D_MODEL))],
        out_specs=[row(Q_DIM), row(4 * LANES), row(D_INNER), row(XBC_DIM), row(LANES), row(2 * D_MODEL)],
        compiler_params=_params(("parallel",)),
        name="proj",
    )(x2, norm_w.reshape(1, -1), wq, wkv, wz, wxbc, wdt, wg, b_gate.reshape(1, -1))


PAIRS_PER_KV = ATTN_HEADS // ATTN_KV_HEADS // 2


SWA_BLOCKS = 4


def _swa_block(sink_ref, q_ref, kv_prev, kv_cur, bias, o_ref):
    w = WINDOW
    upper = lax.broadcasted_iota(I32, (2 * w, LANES), 1) >= HEAD_DIM
    zero = jnp.zeros((2 * w, LANES), BF)
    ones = jnp.ones((2 * w, LANES), BF)
    chains = [(kvh, r) for kvh in range(ATTN_KV_HEADS) for r in range(2)]
    half = lambda t, r: jnp.where(upper, t, zero) if r else jnp.where(upper, zero, t)
    scores, values, sinks = [], [], []
    for kvh, r in chains:
        ks = slice(LANES * kvh, LANES * (kvh + 1))
        vs = slice(LANES * (ATTN_KV_HEADS + kvh), LANES * (ATTN_KV_HEADS + kvh + 1))
        k2 = jnp.concatenate([kv_prev[:, ks], kv_cur[:, ks]], axis=0)
        v2 = jnp.concatenate([kv_prev[:, vs], kv_cur[:, vs]], axis=0)
        q4 = jnp.concatenate([q_ref[:, LANES * (kvh * PAIRS_PER_KV + p):LANES * (kvh * PAIRS_PER_KV + p + 1)]
                              for p in range(PAIRS_PER_KV)], axis=0)
        scores.append(_dot_nt(q4, half(k2, r)) + bias(kvh, r))
        values.append(jnp.concatenate([half(v2, r), ones], axis=1))
        sinks.append(jnp.concatenate(
            [jnp.full((w, 1), sink_ref[2 * (kvh * PAIRS_PER_KV + p) + r], F32) for p in range(PAIRS_PER_KV)],
            axis=0))
    maxes = [jnp.maximum(jnp.max(s, axis=-1, keepdims=True), sk) for s, sk in zip(scores, sinks)]
    probs = [jnp.exp(s - m).astype(BF) for s, m in zip(scores, maxes)]
    nds = [_dot(p, v) for p, v in zip(probs, values)]
    outs = [nd[:, :LANES] / (nd[:, LANES:] + jnp.exp(sk - m)) for nd, sk, m in zip(nds, sinks, maxes)]
    for kvh in range(ATTN_KV_HEADS):
        o4 = outs[2 * kvh] + outs[2 * kvh + 1]
        for p in range(PAIRS_PER_KV):
            pidx = kvh * PAIRS_PER_KV + p
            o_ref[:, LANES * pidx:LANES * (pidx + 1)] = o4[w * p:w * (p + 1)].astype(BF)


def _swa_kernel(sink_ref, q_ref, kvc_ref, kvp_ref, bias_ref, o_ref):
    has_prev = jnp.minimum(pl.program_id(1), 1)
    for sub in range(SWA_BLOCKS):
        rows = pl.ds(WINDOW * sub, WINDOW)
        kv_prev = kvp_ref if sub == 0 else kvc_ref.at[pl.ds(WINDOW * (sub - 1), WINDOW)]
        variant = has_prev if sub == 0 else 1
        _swa_block(sink_ref, q_ref.at[rows], kv_prev, kvc_ref.at[rows],
                   lambda kvh, r, v=variant: bias_ref[v, kvh, r], o_ref.at[rows])


def _swa_bias():
    qi = jnp.arange(WINDOW)[:, None]
    kj = jnp.arange(2 * WINDOW)[None, :]
    dist = qi + WINDOW - kj
    in_window = (dist >= 0) & (dist < WINDOW)
    heads = jnp.arange(ATTN_HEADS).reshape(ATTN_KV_HEADS, PAIRS_PER_KV, 2).transpose(0, 2, 1)
    slopes = jnp.exp2(-8.0 * (heads + 1).astype(F32) / ATTN_HEADS)
    bias = -slopes[..., None, None] * dist.astype(F32)
    tabs = []
    for has_prev in (False, True):
        mask = in_window & (has_prev | (kj >= WINDOW))
        tabs.append(jnp.where(mask, bias, NEG_BIG).reshape(ATTN_KV_HEADS, 2, PAIRS_PER_KV * WINDOW, 2 * WINDOW))
    return jnp.stack(tabs)


def _swa(q, kv, sinks, bsz, seq):
    nb = seq // (WINDOW * SWA_BLOCKS)
    n = bsz * seq
    rows = WINDOW * SWA_BLOCKS
    return pl.pallas_call(
        _swa_kernel,
        out_shape=jax.ShapeDtypeStruct((n, Q_DIM), BF),
        grid=(bsz, nb),
        in_specs=[
            pl.BlockSpec(memory_space=pltpu.SMEM),
            pl.BlockSpec((rows, Q_DIM), lambda b, j: (b * nb + j, 0)),
            pl.BlockSpec((rows, 4 * LANES), lambda b, j: (b * nb + j, 0)),
            pl.BlockSpec((WINDOW, 4 * LANES), lambda b, j: (jnp.maximum((b * nb + j) * SWA_BLOCKS - 1, 0), 0)),
            _full((2, ATTN_KV_HEADS, 2, PAIRS_PER_KV * WINDOW, 2 * WINDOW)),
        ],
        out_specs=pl.BlockSpec((rows, Q_DIM), lambda b, j: (b * nb + j, 0)),
        compiler_params=_params(("parallel", "parallel")),
        name="swa",
    )(sinks.astype(F32), q, kv, kv, _swa_bias())


def _silu(x):
    hx = 0.5 * x
    return hx + hx * jnp.tanh(hx)


def _softplus(x):
    return jnp.maximum(x, 0.0) + jnp.log1p(jnp.exp(-jnp.abs(x)))


SSD_CHUNKS = 2


def _ssd_chunk(xbc_ref, z_ref, dtc_ref, dt_row_raw, cw_ref, cb_ref, dtbc_ref, dtbr_ref, alc_ref, alr_ref,
               dskip_ref, nw_ref, o_ref, state_ref, tail_ref):
    L = CHUNK
    x_raw = xbc_ref[...].astype(F32)
    prev8 = tail_ref[...]
    row8 = lax.broadcasted_iota(I32, (SUBLANES, XBC_DIM), 0)
    acc = x_raw * cw_ref[CONV_WIDTH - 1:CONV_WIDTH, :] + cb_ref[...]
    for s in range(1, CONV_WIDTH):
        sh = pltpu.roll(x_raw, s, axis=0)
        top = jnp.where(row8 < s, pltpu.roll(prev8, s, axis=0), sh[:SUBLANES])
        sh = jnp.concatenate([top, sh[SUBLANES:]], axis=0)
        acc = acc + sh * cw_ref[CONV_WIDTH - 1 - s:CONV_WIDTH - s, :]
    tail_ref[...] = x_raw[L - SUBLANES:, :]
    xc = _silu(acc)

    dt_col = _softplus(dtc_ref[...] + dtbc_ref[...])
    dt_row = _softplus(dt_row_raw + dtbr_ref[...])
    a_col = dt_col * (-jnp.exp(alc_ref[...]))
    a_row = dt_row * (-jnp.exp(alr_ref[...]))
    ii = lax.broadcasted_iota(I32, (L, L), 0)
    jj = lax.broadcasted_iota(I32, (L, L), 1)
    causal = ii >= jj
    tri_l = jnp.where(causal, 1.0, 0.0).astype(F32)
    tri_u = jnp.where(ii <= jj, 1.0, 0.0).astype(F32)
    acum_col = _dot_f32(tri_l, a_col)
    acum_row = _dot_f32(a_row, tri_u)
    decay_col = jnp.exp(acum_col)
    last_col = decay_col[L - 1:L, :]
    last_row = acum_row[:, L - 1:L]
    wrow = jnp.exp(last_row - acum_row) * dt_row
    shifted_row = acum_row - jnp.log(dt_row)

    upper = lax.broadcasted_iota(I32, (L, LANES), 1) >= D_INNER // SSM_HEADS
    upper1 = upper[:1]
    lower = jnp.logical_not(upper)
    pairs_per_group = SSM_HEADS // SSM_GROUPS // 2
    group_w = D_INNER // SSM_GROUPS
    for g in range(SSM_GROUPS):
        b_g = xc[:, D_INNER + D_STATE * g:D_INNER + D_STATE * (g + 1)]
        c_g = xc[:, D_INNER + D_STATE * (SSM_GROUPS + g):D_INNER + D_STATE * (SSM_GROUPS + g + 1)]
        c_bf = c_g.astype(BF)
        cb = _dot_nt(c_bf, b_g.astype(BF))
        b_gt = b_g.astype(F32).T
        pairs = range(g * pairs_per_group, (g + 1) * pairs_per_group)
        states = [state_ref[p] for p in pairs]
        y_off = _dot(c_bf, jnp.concatenate(states, axis=1).astype(BF))
        ys = []
        for j, pidx in enumerate(pairs):
            x_pair = xc[:, LANES * pidx:LANES * (pidx + 1)]
            xms = (jnp.where(lower, x_pair, 0.0).astype(BF), jnp.where(upper, x_pair, 0.0).astype(BF))
            acc_y = None
            acc_s = None
            for r in range(2):
                h = 2 * pidx + r
                acol = jnp.broadcast_to(acum_col[:, h:h + 1], (L, L))
                m_h = cb * jnp.exp(jnp.where(causal, acol - shifted_row[h:h + 1, :], NEG_BIG))
                y = _dot(m_h.astype(BF), xms[r])
                acc_y = y if acc_y is None else acc_y + y
                bw = (b_gt * wrow[h:h + 1, :]).astype(BF)
                sn = _dot(bw, xms[r])
                acc_s = sn if acc_s is None else acc_s + sn
            h0, h1 = 2 * pidx, 2 * pidx + 1
            dec = jnp.where(upper, jnp.broadcast_to(decay_col[:, h1:h1 + 1], (L, LANES)),
                            jnp.broadcast_to(decay_col[:, h0:h0 + 1], (L, LANES)))
            d0 = jnp.broadcast_to(last_col[:, h0:h0 + 1], (1, LANES))
            d1 = jnp.broadcast_to(last_col[:, h1:h1 + 1], (1, LANES))
            state_ref[pidx] = states[j] * jnp.where(upper1, d1, d0) + acc_s
            ys.append(acc_y + y_off[:, LANES * j:LANES * (j + 1)] * dec
                      + x_pair * dskip_ref[:, LANES * pidx:LANES * (pidx + 1)])
        y_g = jnp.concatenate(ys, axis=1)
        z_g = z_ref[:, group_w * g:group_w * (g + 1)].astype(F32)
        y_g = y_g * _silu(z_g)
        y_g = _rms(y_g, nw_ref[:, group_w * g:group_w * (g + 1)])
        o_ref[:, group_w * g:group_w * (g + 1)] = y_g.astype(BF)


def _ssd_kernel(xbc_ref, z_ref, dtc_ref, dtr_ref, cw_ref, cb_ref, dtbc_ref, dtbr_ref, alc_ref, alr_ref,
                dskip_ref, nw_ref, o_ref, state_ref, tail_ref):
    @pl.when(pl.program_id(1) == 0)
    def _():
        state_ref[...] = jnp.zeros_like(state_ref)
        tail_ref[...] = jnp.zeros_like(tail_ref)

    for sub in range(SSD_CHUNKS):
        rows = pl.ds(CHUNK * sub, CHUNK)
        _ssd_chunk(xbc_ref.at[rows], z_ref.at[rows], dtc_ref.at[rows], dtr_ref[:, CHUNK * sub:CHUNK * (sub + 1)],
                   cw_ref, cb_ref, dtbc_ref, dtbr_ref, alc_ref, alr_ref, dskip_ref, nw_ref, o_ref.at[rows],
                   state_ref, tail_ref)


def _ssd(xbc, z, dt_raw, conv_w, conv_b, dt_bias, a_log, d_skip, norm_w, bsz, seq):
    rows = CHUNK * SSD_CHUNKS
    nc = seq // rows
    n = bsz * seq
    pad = LANES - SSM_HEADS
    dt_row = dt_raw[:, :SSM_HEADS].T
    dtb_col = jnp.pad(dt_bias, (0, pad)).reshape(1, LANES)
    al_col = jnp.pad(a_log, (0, pad)).reshape(1, LANES)
    dtb_row = jnp.broadcast_to(dt_bias[:, None], (SSM_HEADS, CHUNK))
    al_row = jnp.broadcast_to(a_log[:, None], (SSM_HEADS, CHUNK))
    dskip = jnp.repeat(d_skip, D_INNER // SSM_HEADS).reshape(1, D_INNER)
    blk = lambda w: pl.BlockSpec((rows, w), lambda b, j: (b * nc + j, 0))
    return pl.pallas_call(
        _ssd_kernel,
        out_shape=jax.ShapeDtypeStruct((n, D_INNER), BF),
        grid=(bsz, nc),
        in_specs=[
            blk(XBC_DIM), blk(D_INNER), blk(LANES),
            pl.BlockSpec((SSM_HEADS, rows), lambda b, j: (0, b * nc + j)),
            _full((CONV_WIDTH, XBC_DIM)), _full((1, XBC_DIM)),
            _full((1, LANES)), _full((SSM_HEADS, CHUNK)), _full((1, LANES)), _full((SSM_HEADS, CHUNK)),
            _full((1, D_INNER)), _full((1, D_INNER)),
        ],
        out_specs=blk(D_INNER),
        scratch_shapes=[pltpu.VMEM((SSM_HEADS // 2, D_STATE, LANES), F32),
                        pltpu.VMEM((SUBLANES, XBC_DIM), F32)],
        compiler_params=_params(("parallel", "arbitrary")),
        name="ssd",
    )(xbc, z, dt_raw, dt_row, conv_w, conv_b.reshape(1, -1), dtb_col, dtb_row, al_col, al_row,
      dskip, norm_w.reshape(1, -1))


def _outproj_kernel(x_ref, attn_ref, ssm_ref, g_ref, wa_ref, ws_ref, wo_ref, h_ref):
    a = _dot(attn_ref[...], wa_ref[...])
    s = _dot(ssm_ref[...], ws_ref[...])
    merged = g_ref[:, :D_MODEL].astype(F32) * a + g_ref[:, D_MODEL:].astype(F32) * s
    h_ref[...] = x_ref[...] + _dot(merged.astype(BF), wo_ref[...])


def _outproj(x2, attn, ssm, gates, w_attn_o, w_ssm_o, w_out):
    n = x2.shape[0]
    tm = OUT_TM
    row = lambda w: pl.BlockSpec((tm, w), lambda i: (i, 0))
    return pl.pallas_call(
        _outproj_kernel,
        out_shape=jax.ShapeDtypeStruct((n, D_MODEL), F32),
        grid=(n // tm,),
        in_specs=[row(D_MODEL), row(Q_DIM), row(D_INNER), row(2 * D_MODEL),
                  _full((Q_DIM, D_MODEL)), _full((D_INNER, D_MODEL)), _full((D_MODEL, D_MODEL))],
        out_specs=row(D_MODEL),
        compiler_params=_params(("parallel",)),
        name="outproj",
    )(x2, attn, ssm, gates, w_attn_o.astype(BF), w_ssm_o.astype(BF), w_out.astype(BF))


def _memkv_kernel(mem_ref, nw_ref, w_ref, kv_ref):
    mn = _rms(mem_ref[...], nw_ref[...]).astype(BF)
    kv_ref[...] = _dot(mn, w_ref[...]).astype(BF)


def _memkv(mem2, norm_w, w_ckv, bsz):
    return pl.pallas_call(
        _memkv_kernel,
        out_shape=jax.ShapeDtypeStruct((bsz * MEM_TOKENS, 2 * D_MODEL), BF),
        grid=(bsz,),
        in_specs=[pl.BlockSpec((MEM_TOKENS, D_MODEL), lambda b: (b, 0)), _full((1, D_MODEL)),
                  _full((D_MODEL, 2 * D_MODEL))],
        out_specs=pl.BlockSpec((MEM_TOKENS, 2 * D_MODEL), lambda b: (b, 0)),
        compiler_params=_params(("parallel",)),
        name="memkv",
    )(mem2, norm_w.reshape(1, -1), w_ckv.astype(BF))


def _cross_kernel(h_ref, kv_ref, nw_ref, wq_ref, wo_ref, o_ref):
    h = h_ref[...]
    hn = _rms(h, nw_ref[...]).astype(BF)
    q = (_dot(hn, wq_ref[...]) * (CROSS_HEAD_DIM ** -0.5)).astype(BF)
    outs = []
    for hd in range(CROSS_HEADS):
        lo, hi = CROSS_HEAD_DIM * hd, CROSS_HEAD_DIM * (hd + 1)
        s = _dot_nt(q[:, lo:hi], kv_ref[:, lo:hi])
        p = jnp.exp(s - jnp.max(s, axis=-1, keepdims=True))
        den = jnp.sum(p, axis=-1, keepdims=True)
        outs.append((_dot(p.astype(BF), kv_ref[:, D_MODEL + lo:D_MODEL + hi]) / den).astype(BF))
    o = jnp.concatenate(outs, axis=1)
    o_ref[...] = h + _dot(o, wo_ref[...])


def _cross(h, kv, norm_w, w_cq, w_co, seq):
    n = h.shape[0]
    tm = CROSS_TM
    per_b = seq // tm
    row = pl.BlockSpec((tm, D_MODEL), lambda i: (i, 0))
    return pl.pallas_call(
        _cross_kernel,
        out_shape=jax.ShapeDtypeStruct((n, D_MODEL), F32),
        grid=(n // tm,),
        in_specs=[row, pl.BlockSpec((MEM_TOKENS, 2 * D_MODEL), lambda i: (i // per_b, 0)),
                  _full((1, D_MODEL)), _full((D_MODEL, D_MODEL)), _full((D_MODEL, D_MODEL))],
        out_specs=row,
        compiler_params=_params(("parallel",)),
        name="cross",
    )(h, kv, norm_w.reshape(1, -1), w_cq.astype(BF), w_co.astype(BF))


def _router_kernel(h_ref, nw_ref, wr_ref, br_ref, t_ref, meta_ref, cnt_ref, carry_ref):
    i = pl.program_id(0)
    tm = ROUTER_TM

    @pl.when(i == 0)
    def _():
        carry_ref[...] = jnp.zeros_like(carry_ref)

    t = _rms(h_ref[...], nw_ref[...])
    for c in range(ROW_TILES):
        t_ref[pl.ds(c, tm, stride=ROW_TILES), :] = t[:, LANES * c:LANES * (c + 1)]
    lane = lax.broadcasted_iota(I32, (tm, LANES), 1)
    t_hi = t.astype(BF)
    t_lo = (t - t_hi.astype(F32)).astype(BF)
    w_hi = wr_ref[...].astype(BF)
    w_lo = (wr_ref[...] - w_hi.astype(F32)).astype(BF)
    raw = _dot(t_hi, w_hi) + (_dot(t_lo, w_hi) + _dot(t_hi, w_lo))
    logits = jnp.where(lane < N_EXPERTS, raw + br_ref[...], NEG_BIG)
    work = logits
    vals, ids, hots = [], [], []
    for _k in range(TOP_K):
        m = jnp.max(work, axis=-1, keepdims=True)
        idx = jnp.min(jnp.where(work == m, lane, LANES), axis=-1, keepdims=True)
        hot = lane == idx
        vals.append(m)
        ids.append(idx)
        hots.append(hot)
        work = jnp.where(hot, NEG_BIG * 2.0, work)
    es = [jnp.exp(v - vals[0]) for v in vals]
    den = es[0] + es[1] + es[2] + es[3]
    member = jnp.zeros((tm, LANES), F32)
    for hot in hots:
        member = member + jnp.where(hot, 1.0, 0.0)
    ii = lax.broadcasted_iota(I32, (tm, tm), 0)
    jj = lax.broadcasted_iota(I32, (tm, tm), 1)
    strict = jnp.where(ii > jj, 1.0, 0.0).astype(BF)
    carry = carry_ref[0:1, :]
    before = _dot(strict, member.astype(BF)) + carry
    meta = jnp.zeros((tm, LANES), F32)
    for k in range(TOP_K):
        rank = jnp.sum(jnp.where(hots[k], before, 0.0), axis=-1, keepdims=True)
        meta = jnp.where(lane == k, ids[k].astype(F32), meta)
        meta = jnp.where(lane == TOP_K + k, rank, meta)
        meta = jnp.where(lane == 2 * TOP_K + k, es[k] / den, meta)
    meta_ref[...] = meta
    new_carry = carry + jnp.sum(member, axis=0, keepdims=True)
    carry_ref[...] = jnp.broadcast_to(new_carry, carry_ref.shape)
    cnt_ref[...] = jnp.broadcast_to(new_carry, cnt_ref.shape)


def _router(h, norm_w, w_router, b_router):
    n = h.shape[0]
    tm = ROUTER_TM
    wr = jnp.pad(w_router, ((0, 0), (0, LANES - N_EXPERTS)))
    br = jnp.pad(b_router, (0, LANES - N_EXPERTS)).reshape(1, LANES)
    return pl.pallas_call(
        _router_kernel,
        out_shape=(jax.ShapeDtypeStruct((n * ROW_TILES, LANES), F32),
                   jax.ShapeDtypeStruct((n, LANES), F32),
                   jax.ShapeDtypeStruct((SUBLANES, LANES), F32)),
        grid=(n // tm,),
        in_specs=[pl.BlockSpec((tm, D_MODEL), lambda i: (i, 0)), _full((1, D_MODEL)),
                  _full((D_MODEL, LANES)), _full((1, LANES))],
        out_specs=[pl.BlockSpec((tm * ROW_TILES, LANES), lambda i: (i, 0)),
                   pl.BlockSpec((tm, LANES), lambda i: (i, 0)),
                   pl.BlockSpec((SUBLANES, LANES), lambda i: (0, 0))],
        scratch_shapes=[pltpu.VMEM((SUBLANES, LANES), F32)],
        compiler_params=_params(("arbitrary",)),
        name="router",
    )(h, norm_w.reshape(1, -1), wr, br)


def _row(ref, r):
    return ref.at[pl.ds(pl.multiple_of(r * ROW_TILES, ROW_TILES), ROW_TILES)]


def _dispatch_kernel(dest_ref, t_ref, xs_hbm, sem):
    tt = DISPATCH_TT

    def copy(j, k):
        return pltpu.make_async_copy(_row(t_ref, j), _row(xs_hbm, dest_ref[j * TOP_K + k]), sem)

    def issue(j, carry):
        for k in range(TOP_K):
            copy(j, k).start(priority=k % DMA_PRIORITIES)
        return carry

    def drain(j, carry):
        for k in range(TOP_K):
            copy(j, k).wait()
        return carry

    lax.fori_loop(0, tt, issue, 0)
    lax.fori_loop(0, tt, drain, 0)


def _dispatch(t_rows, dest):
    n = t_rows.shape[0] // ROW_TILES
    tt = DISPATCH_TT
    return pl.pallas_call(
        _dispatch_kernel,
        out_shape=jax.ShapeDtypeStruct((n * TOP_K * ROW_TILES, LANES), F32),
        grid=(n // tt,),
        in_specs=[pl.BlockSpec((tt * TOP_K,), lambda i: (i,), memory_space=pltpu.SMEM),
                  pl.BlockSpec((tt * ROW_TILES, LANES), lambda i: (i, 0))],
        out_specs=pl.BlockSpec(memory_space=pl.ANY),
        scratch_shapes=[pltpu.SemaphoreType.DMA],
        compiler_params=_params(("arbitrary",)),
        name="dispatch",
    )(dest, t_rows)


def _expert_kernel(tile_ref, exp_ref, lo_ref, hi_ref, newexp_ref, slot_ref, next_ref,
                   xs_ref, wgu_hbm, bgu_ref, wdn_hbm, bdn_ref, ys_ref, wgu_bf, wdn_bf, wgu_f32, wdn_f32, sems):
    w = pl.program_id(0)
    tm = EXPERT_TM
    lo = lo_ref[w]
    hi = hi_ref[w]

    def weight_copies(e, slot):
        return (pltpu.make_async_copy(wgu_hbm.at[e], wgu_f32.at[slot], sems.at[0, slot]),
                pltpu.make_async_copy(wdn_hbm.at[e], wdn_f32.at[slot], sems.at[1, slot]))

    @pl.when(w == 0)
    def _():
        for cp in weight_copies(exp_ref[0], 0):
            cp.start()

    @pl.when(newexp_ref[w] == 1)
    def _():
        slot = slot_ref[w]
        for cp in weight_copies(exp_ref[w], slot):
            cp.wait()
        wgu_bf[...] = wgu_f32[slot].astype(BF)
        wdn_bf[...] = wdn_f32[slot].astype(BF)

        @pl.when(next_ref[w] >= 0)
        def _():
            for cp in weight_copies(next_ref[w], 1 - slot):
                cp.start()

    sb = EXPERT_SUB
    for part in range(tm // sb):
        start = tile_ref[w] * tm + part * sb

        @pl.when(jnp.minimum(hi, start + sb) > jnp.maximum(lo, start))
        def _(part=part, start=start):
            chunk = lambda c: pl.ds(part * sb * ROW_TILES + c, sb, stride=ROW_TILES)
            x = jnp.concatenate([xs_ref[chunk(c), :] for c in range(ROW_TILES)], axis=1).astype(BF)
            gu = _dot(x, wgu_bf[...]) + bgu_ref[0]
            g = jnp.minimum(gu[:, :D_EXPERT], SWIGLU_LIMIT)
            up = jnp.clip(gu[:, D_EXPERT:], -SWIGLU_LIMIT, SWIGLU_LIMIT)
            hg = 0.5 * g
            act = (up + 1.0) * (hg + hg * jnp.tanh(SWIGLU_ALPHA * hg))
            y = _dot(act.astype(BF), wdn_bf[...]) + bdn_ref[0]
            rows = start + lax.broadcasted_iota(I32, (sb, LANES), 0)
            mine = (rows >= lo) & (rows < hi)
            is_first = lo <= start

            @pl.when(is_first)
            def _():
                for c in range(ROW_TILES):
                    ys_ref[chunk(c), :] = jnp.where(mine, y[:, LANES * c:LANES * (c + 1)], 0.0)

            @pl.when(jnp.logical_not(is_first))
            def _():
                for c in range(ROW_TILES):
                    ys_ref[chunk(c), :] = jnp.where(mine, y[:, LANES * c:LANES * (c + 1)], ys_ref[chunk(c), :])


def _experts(xs, items, w_gu, b_gu, w_dn, b_dn):
    n_items = items[0].shape[0]
    tm = EXPERT_TM
    grid_spec = pltpu.PrefetchScalarGridSpec(
        num_scalar_prefetch=len(items),
        grid=(n_items,),
        in_specs=[
            pl.BlockSpec((tm * ROW_TILES, LANES), lambda w, t, e, *_: (t[w], 0)),
            pl.BlockSpec(memory_space=pl.ANY),
            pl.BlockSpec((1, 1, 2 * D_EXPERT), lambda w, t, e, *_: (e[w], 0, 0)),
            pl.BlockSpec(memory_space=pl.ANY),
            pl.BlockSpec((1, 1, D_MODEL), lambda w, t, e, *_: (e[w], 0, 0)),
        ],
        out_specs=pl.BlockSpec((tm * ROW_TILES, LANES), lambda w, t, e, *_: (t[w], 0)),
        scratch_shapes=[pltpu.VMEM((D_MODEL, 2 * D_EXPERT), BF), pltpu.VMEM((D_EXPERT, D_MODEL), BF),
                        pltpu.VMEM((2, D_MODEL, 2 * D_EXPERT), F32), pltpu.VMEM((2, D_EXPERT, D_MODEL), F32),
                        pltpu.SemaphoreType.DMA((2, 2))],
    )
    return pl.pallas_call(
        _expert_kernel,
        out_shape=jax.ShapeDtypeStruct(xs.shape, F32),
        grid_spec=grid_spec,
        compiler_params=_params(("arbitrary",)),
        name="experts",
    )(*items, xs, w_gu, b_gu.reshape(N_EXPERTS, 1, -1), w_dn, b_dn.reshape(N_EXPERTS, 1, -1))


def _expert_items(counts, n_rows):
    tm = EXPERT_TM
    n_tiles = n_rows // tm
    n_items = n_tiles + N_EXPERTS - 1
    end = jnp.cumsum(counts)
    off = end - counts
    first_tile = off // tm
    last_tile = jnp.maximum(end - 1, 0) // tm
    per = jnp.where(counts > 0, last_tile - first_tile + 1, 0)
    cum = jnp.cumsum(per)
    start = cum - per
    total = cum[-1]
    w = jnp.arange(n_items, dtype=I32)
    wc = jnp.minimum(w, total - 1)
    e_of = jnp.minimum(jnp.sum((cum[None, :] <= wc[:, None]).astype(I32), axis=1), N_EXPERTS - 1)
    sel = e_of[:, None] == jnp.arange(N_EXPERTS, dtype=I32)
    pick = lambda v: jnp.sum(jnp.where(sel, v[None, :], 0), axis=1)
    t_of = (pick(first_tile) + (wc - pick(start))).astype(I32)
    valid = w < total
    lo = jnp.where(valid, jnp.maximum(pick(off), t_of * tm), 0).astype(I32)
    hi = jnp.where(valid, jnp.minimum(pick(end), (t_of + 1) * tm), 0).astype(I32)
    prev_e = jnp.concatenate([jnp.full((1,), -1, I32), e_of[:-1]])
    newexp = (e_of != prev_e).astype(I32)
    slot = (jnp.cumsum(newexp) - 1) % 2
    later_change = (w[None, :] > w[:, None]) & (newexp[None, :] == 1)
    nxt = jnp.min(jnp.where(later_change, w[None, :], n_items), axis=1)
    next_e = jnp.sum(jnp.where(nxt[:, None] == w[None, :], e_of[None, :], 0), axis=1)
    next_e = jnp.where(nxt < n_items, next_e, -1).astype(I32)
    return t_of, e_of, lo, hi, newexp, slot.astype(I32), next_e


def _combine_kernel(dest_ref, meta_ref, h_ref, fw_ref, ys_hbm, o_ref, buf_ref, sems):
    tt = COMBINE_TT
    tp = tt // COMBINE_PHASES

    def copy(j, k, p):
        return pltpu.make_async_copy(_row(ys_hbm, dest_ref[j * TOP_K + k]), _row(buf_ref, k * tt + j), sems.at[p])

    for p in range(COMBINE_PHASES):
        def issue(j, carry, p=p):
            for k in range(TOP_K):
                copy(j, k, p).start(priority=k % DMA_PRIORITIES)
            return carry
        lax.fori_loop(p * tp, (p + 1) * tp, issue, 0)

    for p in range(COMBINE_PHASES):
        def drain(j, carry, p=p):
            for k in range(TOP_K):
                copy(j, k, p).wait()
            return carry
        lax.fori_loop(p * tp, (p + 1) * tp, drain, 0)
        rows = pl.ds(p * tp, tp)
        meta = meta_ref[rows, :]
        wts = [jnp.broadcast_to(meta[:, 2 * TOP_K + k:2 * TOP_K + k + 1], (tp, LANES)) for k in range(TOP_K)]
        cols = []
        for c in range(ROW_TILES):
            acc = h_ref[rows, LANES * c:LANES * (c + 1)]
            for k in range(TOP_K):
                acc = acc + wts[k] * buf_ref[pl.ds((k * tt + p * tp) * ROW_TILES + c, tp, stride=ROW_TILES), :]
            cols.append(acc)
        o_ref[rows, :] = _rms(jnp.concatenate(cols, axis=1), fw_ref[...])


def _combine(dest, meta, h, final_w, ys):
    n = h.shape[0]
    tt = COMBINE_TT
    return pl.pallas_call(
        _combine_kernel,
        out_shape=jax.ShapeDtypeStruct((n, D_MODEL), F32),
        grid=(n // tt,),
        in_specs=[pl.BlockSpec((tt * TOP_K,), lambda i: (i,), memory_space=pltpu.SMEM),
                  pl.BlockSpec((tt, LANES), lambda i: (i, 0)),
                  pl.BlockSpec((tt, D_MODEL), lambda i: (i, 0)),
                  _full((1, D_MODEL)),
                  pl.BlockSpec(memory_space=pl.ANY)],
        out_specs=pl.BlockSpec((tt, D_MODEL), lambda i: (i, 0)),
        scratch_shapes=[pltpu.VMEM((TOP_K * tt * ROW_TILES, LANES), F32),
                        pltpu.SemaphoreType.DMA((COMBINE_PHASES,))],
        compiler_params=_params(("arbitrary",)),
        name="combine",
    )(dest, meta, h, final_w.reshape(1, -1), ys)


def _moe(h, norm_w, w_router, b_router, w_gu, b_gu, w_dn, b_dn, final_w):
    n = h.shape[0]
    t_rows, meta, cnt = _router(h, norm_w, w_router, b_router)
    eid = meta[:, :TOP_K].astype(I32)
    rank = meta[:, TOP_K:2 * TOP_K].astype(I32)
    counts = cnt[0, :N_EXPERTS].astype(I32)
    off = jnp.cumsum(counts) - counts
    hot = eid[..., None] == jnp.arange(N_EXPERTS, dtype=I32)
    dest = (jnp.sum(jnp.where(hot, off, 0), axis=-1) + rank).reshape(-1)
    xs = _dispatch(t_rows, dest)
    ys = _experts(xs, _expert_items(counts, n * TOP_K), w_gu, b_gu, w_dn, b_dn)
    return _combine(dest, meta, h, final_w, ys)


def kernel(x, mem, norm_mix_w, w_in, w_gate, b_gate, attn_sinks, conv_w, conv_b, dt_bias, a_log, d_skip,
           ssm_norm_w, w_attn_o, w_ssm_o, w_out, norm_cross_w, norm_mem_w, w_cq, w_ckv, w_co,
           norm_ffn_w, w_router, b_router, w_gu, b_gu, w_dn, b_dn, final_norm_w):
    bsz, seq, _ = x.shape
    assert norm_mix_w.shape[0] == 1, "single-layer block"
    x2 = x.reshape(bsz * seq, D_MODEL)
    q, kv, z, xbc, dt_raw, gates = _proj(x2, norm_mix_w[0], w_in[0], w_gate[0], b_gate[0])
    attn = _swa(q, kv, attn_sinks[0], bsz, seq)
    ssm = _ssd(xbc, z, dt_raw, conv_w[0], conv_b[0], dt_bias[0], a_log[0], d_skip[0], ssm_norm_w[0], bsz, seq)
    h = _outproj(x2, attn, ssm, gates, w_attn_o[0], w_ssm_o[0], w_out[0])
    mkv = _memkv(mem.reshape(bsz * MEM_TOKENS, D_MODEL), norm_mem_w[0], w_ckv[0], bsz)
    h = _cross(h, mkv, norm_cross_w[0], w_cq[0], w_co[0], seq)
    out = _moe(h, norm_ffn_w[0], w_router[0], b_router[0], w_gu[0], b_gu[0], w_dn[0], b_dn[0], final_norm_w)
    return out.reshape(bsz, seq, D_MODEL)
```

```python
import functools

import jax
import jax.numpy as jnp
from jax import lax
from jax.experimental import pallas as pl
from jax.experimental.pallas import tpu as pltpu

BF = jnp.bfloat16
F32 = jnp.float32
I32 = jnp.int32

D_MODEL = 1024
RMS_EPS = 1e-5
ATTN_HEADS = 16
ATTN_KV_HEADS = 2
HEAD_DIM = 64
WINDOW = 128
Q_DIM = ATTN_HEADS * HEAD_DIM
KV_DIM = ATTN_KV_HEADS * HEAD_DIM
D_INNER = 2 * D_MODEL
SSM_HEADS = 32
SSM_GROUPS = 4
D_STATE = 128
CONV_WIDTH = 4
CHUNK = 128
XBC_DIM = D_INNER + 2 * SSM_GROUPS * D_STATE
MEM_TOKENS = 256
CROSS_HEADS = 4
CROSS_HEAD_DIM = D_MODEL // CROSS_HEADS
N_EXPERTS = 32
TOP_K = 4
D_EXPERT = D_MODEL
SWIGLU_LIMIT = 7.0
SWIGLU_ALPHA = 1.702

LANES = 128
SUBLANES = 8
ROW_TILES = D_MODEL // LANES
NEG_BIG = -1e30
VMEM_LIMIT = 56 * 1024 * 1024

PROJ_TM = 512
DMA_PRIORITIES = 2
OUT_TM = 512
CROSS_TM = 512
ROUTER_TM = 512
DISPATCH_TT = 512
EXPERT_TM = 512
EXPERT_SUB = 256
COMBINE_TT = 512
COMBINE_PHASES = 4


def _rms(x, w):
    return x * lax.rsqrt(jnp.mean(x * x, axis=-1, keepdims=True) + RMS_EPS) * w


def _dot(a, b):
    return jnp.dot(a, b, preferred_element_type=F32)


def _dot_nt(a, b):
    return lax.dot_general(a, b, (((1,), (1,)), ((), ())), preferred_element_type=F32)


def _dot_f32(a, b):
    return jnp.dot(a, b, preferred_element_type=F32, precision=lax.Precision.HIGHEST)


def _params(sem):
    return pltpu.CompilerParams(dimension_semantics=sem, vmem_limit_bytes=VMEM_LIMIT)


def _full(shape):
    nd = len(shape)
    return pl.BlockSpec(shape, lambda *_: (0,) * nd, pipeline_mode=pl.Buffered(1))


def _proj_kernel(x_ref, nw_ref, wq_ref, wkv_ref, wz_ref, wxbc_ref, wdt_ref, wg_ref, bg_ref,
                 q_ref, kv_ref, z_ref, xbc_ref, dt_ref, g_ref):
    u = _rms(x_ref[...], nw_ref[...]).astype(BF)
    q_ref[...] = _dot(u, wq_ref[...]).astype(BF)
    kv_ref[...] = _dot(u, wkv_ref[...]).astype(BF)
    z_ref[...] = _dot(u, wz_ref[...]).astype(BF)
    xbc_ref[...] = _dot(u, wxbc_ref[...]).astype(BF)
    dt_ref[...] = _dot(u, wdt_ref[...])
    g_ref[...] = jax.nn.sigmoid(_dot(u, wg_ref[...]) + bg_ref[...]).astype(BF)


def _proj(x2, norm_w, w_in, w_gate, b_gate):
    n = x2.shape[0]
    tm = PROJ_TM
    s0, s1, s2, s3, s4 = Q_DIM, Q_DIM + KV_DIM, Q_DIM + 2 * KV_DIM, Q_DIM + 2 * KV_DIM + D_INNER, \
        Q_DIM + 2 * KV_DIM + D_INNER + XBC_DIM
    wq = (w_in[:, :s0] * (HEAD_DIM ** -0.5)).astype(BF)
    wk, wv = w_in[:, s0:s1], w_in[:, s1:s2]
    dup = lambda w: jnp.concatenate([w[:, :HEAD_DIM], w[:, :HEAD_DIM], w[:, HEAD_DIM:], w[:, HEAD_DIM:]], axis=1)
    wkv = jnp.concatenate([dup(wk), dup(wv)], axis=1).astype(BF)
    wz = w_in[:, s2:s3].astype(BF)
    wxbc = w_in[:, s3:s4].astype(BF)
    wdt = jnp.pad(w_in[:, s4:], ((0, 0), (0, LANES - SSM_HEADS))).astype(BF)
    wg = w_gate.astype(BF)
    outs = (
        jax.ShapeDtypeStruct((n, Q_DIM), BF),
        jax.ShapeDtypeStruct((n, 4 * LANES), BF),
        jax.ShapeDtypeStruct((n, D_INNER), BF),
        jax.ShapeDtypeStruct((n, XBC_DIM), BF),
        jax.ShapeDtypeStruct((n, LANES), F32),
        jax.ShapeDtypeStruct((n, 2 * D_MODEL), BF),
    )
    row = lambda w: pl.BlockSpec((tm, w), lambda i: (i, 0))
    return pl.pallas_call(
        _proj_kernel,
        out_shape=outs,
        grid=(n // tm,),
        in_specs=[row(D_MODEL), _full((1, D_MODEL)), _full(wq.shape), _full(wkv.shape), _full(wz.shape),
                  _full(wxbc.shape), _full(wdt.shape), _full(wg.shape), _full((1, 2 * D_MODEL))],
        out_specs=[row(Q_DIM), row(4 * LANES), row(D_INNER), row(XBC_DIM), row(LANES), row(2 * D_MODEL)],
        compiler_params=_params(("parallel",)),
        name="proj",
    )(x2, norm_w.reshape(1, -1), wq, wkv, wz, wxbc, wdt, wg, b_gate.reshape(1, -1))


PAIRS_PER_KV = ATTN_HEADS // ATTN_KV_HEADS // 2


SWA_BLOCKS = 4


def _swa_block(sink_ref, q_ref, kv_prev, kv_cur, bias, o_ref):
    w = WINDOW
    upper = lax.broadcasted_iota(I32, (2 * w, LANES), 1) >= HEAD_DIM
    zero = jnp.zeros((2 * w, LANES), BF)
    ones = jnp.ones((2 * w, LANES), BF)
    chains = [(kvh, r) for kvh in range(ATTN_KV_HEADS) for r in range(2)]
    half = lambda t, r: jnp.where(upper, t, zero) if r else jnp.where(upper, zero, t)
    scores, values, sinks = [], [], []
    for kvh, r in chains:
        ks = slice(LANES * kvh, LANES * (kvh + 1))
        vs = slice(LANES * (ATTN_KV_HEADS + kvh), LANES * (ATTN_KV_HEADS + kvh + 1))
        k2 = jnp.concatenate([kv_prev[:, ks], kv_cur[:, ks]], axis=0)
        v2 = jnp.concatenate([kv_prev[:, vs], kv_cur[:, vs]], axis=0)
        q4 = jnp.concatenate([q_ref[:, LANES * (kvh * PAIRS_PER_KV + p):LANES * (kvh * PAIRS_PER_KV + p + 1)]
                              for p in range(PAIRS_PER_KV)], axis=0)
        scores.append(_dot_nt(q4, half(k2, r)) + bias(kvh, r))
        values.append(jnp.concatenate([half(v2, r), ones], axis=1))
        sinks.append(jnp.concatenate(
            [jnp.full((w, 1), sink_ref[2 * (kvh * PAIRS_PER_KV + p) + r], F32) for p in range(PAIRS_PER_KV)],
            axis=0))
    maxes = [jnp.maximum(jnp.max(s, axis=-1, keepdims=True), sk) for s, sk in zip(scores, sinks)]
    probs = [jnp.exp(s - m).astype(BF) for s, m in zip(scores, maxes)]
    nds = [_dot(p, v) for p, v in zip(probs, values)]
    outs = [nd[:, :LANES] / (nd[:, LANES:] + jnp.exp(sk - m)) for nd, sk, m in zip(nds, sinks, maxes)]
    for kvh in range(ATTN_KV_HEADS):
        o4 = outs[2 * kvh] + outs[2 * kvh + 1]
        for p in range(PAIRS_PER_KV):
            pidx = kvh * PAIRS_PER_KV + p
            o_ref[:, LANES * pidx:LANES * (pidx + 1)] = o4[w * p:w * (p + 1)].astype(BF)


def _swa_kernel(sink_ref, q_ref, kvc_ref, kvp_ref, bias_ref, o_ref):
    has_prev = jnp.minimum(pl.program_id(1), 1)
    for sub in range(SWA_BLOCKS):
        rows = pl.ds(WINDOW * sub, WINDOW)
        kv_prev = kvp_ref if sub == 0 else kvc_ref.at[pl.ds(WINDOW * (sub - 1), WINDOW)]
        variant = has_prev if sub == 0 else 1
        _swa_block(sink_ref, q_ref.at[rows], kv_prev, kvc_ref.at[rows],
                   lambda kvh, r, v=variant: bias_ref[v, kvh, r], o_ref.at[rows])


def _swa_bias():
    qi = jnp.arange(WINDOW)[:, None]
    kj = jnp.arange(2 * WINDOW)[None, :]
    dist = qi + WINDOW - kj
    in_window = (dist >= 0) & (dist < WINDOW)
    heads = jnp.arange(ATTN_HEADS).reshape(ATTN_KV_HEADS, PAIRS_PER_KV, 2).transpose(0, 2, 1)
    slopes = jnp.exp2(-8.0 * (heads + 1).astype(F32) / ATTN_HEADS)
    bias = -slopes[..., None, None] * dist.astype(F32)
    tabs = []
    for has_prev in (False, True):
        mask = in_window & (has_prev | (kj >= WINDOW))
        tabs.append(jnp.where(mask, bias, NEG_BIG).reshape(ATTN_KV_HEADS, 2, PAIRS_PER_KV * WINDOW, 2 * WINDOW))
    return jnp.stack(tabs)


def _swa(q, kv, sinks, bsz, seq):
    nb = seq // (WINDOW * SWA_BLOCKS)
    n = bsz * seq
    rows = WINDOW * SWA_BLOCKS
    return pl.pallas_call(
        _swa_kernel,
        out_shape=jax.ShapeDtypeStruct((n, Q_DIM), BF),
        grid=(bsz, nb),
        in_specs=[
            pl.BlockSpec(memory_space=pltpu.SMEM),
            pl.BlockSpec((rows, Q_DIM), lambda b, j: (b * nb + j, 0)),
            pl.BlockSpec((rows, 4 * LANES), lambda b, j: (b * nb + j, 0)),
            pl.BlockSpec((WINDOW, 4 * LANES), lambda b, j: (jnp.maximum((b * nb + j) * SWA_BLOCKS - 1, 0), 0)),
            _full((2, ATTN_KV_HEADS, 2, PAIRS_PER_KV * WINDOW, 2 * WINDOW)),
        ],
        out_specs=pl.BlockSpec((rows, Q_DIM), lambda b, j: (b * nb + j, 0)),
        compiler_params=_params(("parallel", "parallel")),
        name="swa",
    )(sinks.astype(F32), q, kv, kv, _swa_bias())


def _silu(x):
    hx = 0.5 * x
    return hx + hx * jnp.tanh(hx)


def _softplus(x):
    return jnp.maximum(x, 0.0) + jnp.log1p(jnp.exp(-jnp.abs(x)))


SSD_CHUNKS = 4


def _ssd_chunk(xbc_ref, z_ref, dtc_ref, dt_row_raw, cw_ref, cb_ref, dtbc_ref, dtbr_ref, alc_ref, alr_ref,
               dskip_ref, nw_ref, o_ref, state_ref, tail_ref):
    L = CHUNK
    x_raw = xbc_ref[...].astype(F32)
    prev8 = tail_ref[...]
    row8 = lax.broadcasted_iota(I32, (SUBLANES, XBC_DIM), 0)
    acc = x_raw * cw_ref[CONV_WIDTH - 1:CONV_WIDTH, :] + cb_ref[...]
    for s in range(1, CONV_WIDTH):
        sh = pltpu.roll(x_raw, s, axis=0)
        top = jnp.where(row8 < s, pltpu.roll(prev8, s, axis=0), sh[:SUBLANES])
        sh = jnp.concatenate([top, sh[SUBLANES:]], axis=0)
        acc = acc + sh * cw_ref[CONV_WIDTH - 1 - s:CONV_WIDTH - s, :]
    tail_ref[...] = x_raw[L - SUBLANES:, :]
    xc = _silu(acc)

    dt_col = _softplus(dtc_ref[...] + dtbc_ref[...])
    dt_row = _softplus(dt_row_raw + dtbr_ref[...])
    a_col = dt_col * (-jnp.exp(alc_ref[...]))
    a_row = dt_row * (-jnp.exp(alr_ref[...]))
    ii = lax.broadcasted_iota(I32, (L, L), 0)
    jj = lax.broadcasted_iota(I32, (L, L), 1)
    causal = ii >= jj
    tri_l = jnp.where(causal, 1.0, 0.0).astype(F32)
    tri_u = jnp.where(ii <= jj, 1.0, 0.0).astype(F32)
    acum_col = _dot_f32(tri_l, a_col)
    acum_row = _dot_f32(a_row, tri_u)
    decay_col = jnp.exp(acum_col)
    last_col = decay_col[L - 1:L, :]
    last_row = acum_row[:, L - 1:L]
    wrow = jnp.exp(last_row - acum_row) * dt_row
    shifted_row = acum_row - jnp.log(dt_row)

    upper = lax.broadcasted_iota(I32, (L, LANES), 1) >= D_INNER // SSM_HEADS
    upper1 = upper[:1]
    lower = jnp.logical_not(upper)
    pairs_per_group = SSM_HEADS // SSM_GROUPS // 2
    group_w = D_INNER // SSM_GROUPS
    for g in range(SSM_GROUPS):
        b_g = xc[:, D_INNER + D_STATE * g:D_INNER + D_STATE * (g + 1)]
        c_g = xc[:, D_INNER + D_STATE * (SSM_GROUPS + g):D_INNER + D_STATE * (SSM_GROUPS + g + 1)]
        c_bf = c_g.astype(BF)
        cb = _dot_nt(c_bf, b_g.astype(BF))
        b_gt = b_g.astype(F32).T
        pairs = range(g * pairs_per_group, (g + 1) * pairs_per_group)
        states = [state_ref[p] for p in pairs]
        y_off = _dot(c_bf, jnp.concatenate(states, axis=1).astype(BF))
        ys = []
        for j, pidx in enumerate(pairs):
            x_pair = xc[:, LANES * pidx:LANES * (pidx + 1)]
            xms = (jnp.where(lower, x_pair, 0.0).astype(BF), jnp.where(upper, x_pair, 0.0).astype(BF))
            acc_y = None
            acc_s = None
            for r in range(2):
                h = 2 * pidx + r
                acol = jnp.broadcast_to(acum_col[:, h:h + 1], (L, L))
                m_h = cb * jnp.exp(jnp.where(causal, acol - shifted_row[h:h + 1, :], NEG_BIG))
                y = _dot(m_h.astype(BF), xms[r])
                acc_y = y if acc_y is None else acc_y + y
                bw = (b_gt * wrow[h:h + 1, :]).astype(BF)
                sn = _dot(bw, xms[r])
                acc_s = sn if acc_s is None else acc_s + sn
            h0, h1 = 2 * pidx, 2 * pidx + 1
            dec = jnp.where(upper, jnp.broadcast_to(decay_col[:, h1:h1 + 1], (L, LANES)),
                            jnp.broadcast_to(decay_col[:, h0:h0 + 1], (L, LANES)))
            d0 = jnp.broadcast_to(last_col[:, h0:h0 + 1], (1, LANES))
            d1 = jnp.broadcast_to(last_col[:, h1:h1 + 1], (1, LANES))
            state_ref[pidx] = states[j] * jnp.where(upper1, d1, d0) + acc_s
            ys.append(acc_y + y_off[:, LANES * j:LANES * (j + 1)] * dec
                      + x_pair * dskip_ref[:, LANES * pidx:LANES * (pidx + 1)])
        y_g = jnp.concatenate(ys, axis=1)
        z_g = z_ref[:, group_w * g:group_w * (g + 1)].astype(F32)
        y_g = y_g * _silu(z_g)
        y_g = _rms(y_g, nw_ref[:, group_w * g:group_w * (g + 1)])
        o_ref[:, group_w * g:group_w * (g + 1)] = y_g.astype(BF)


def _ssd_kernel(xbc_ref, z_ref, dtc_ref, dtr_ref, cw_ref, cb_ref, dtbc_ref, dtbr_ref, alc_ref, alr_ref,
                dskip_ref, nw_ref, o_ref, state_ref, tail_ref):
    @pl.when(pl.program_id(1) == 0)
    def _():
        state_ref[...] = jnp.zeros_like(state_ref)
        tail_ref[...] = jnp.zeros_like(tail_ref)

    for sub in range(SSD_CHUNKS):
        rows = pl.ds(CHUNK * sub, CHUNK)
        _ssd_chunk(xbc_ref.at[rows], z_ref.at[rows], dtc_ref.at[rows], dtr_ref[:, CHUNK * sub:CHUNK * (sub + 1)],
                   cw_ref, cb_ref, dtbc_ref, dtbr_ref, alc_ref, alr_ref, dskip_ref, nw_ref, o_ref.at[rows],
                   state_ref, tail_ref)


def _ssd(xbc, z, dt_raw, conv_w, conv_b, dt_bias, a_log, d_skip, norm_w, bsz, seq):
    rows = CHUNK * SSD_CHUNKS
    nc = seq // rows
    n = bsz * seq
    pad = LANES - SSM_HEADS
    dt_row = dt_raw[:, :SSM_HEADS].T
    dtb_col = jnp.pad(dt_bias, (0, pad)).reshape(1, LANES)
    al_col = jnp.pad(a_log, (0, pad)).reshape(1, LANES)
    dtb_row = jnp.broadcast_to(dt_bias[:, None], (SSM_HEADS, CHUNK))
    al_row = jnp.broadcast_to(a_log[:, None], (SSM_HEADS, CHUNK))
    dskip = jnp.repeat(d_skip, D_INNER // SSM_HEADS).reshape(1, D_INNER)
    blk = lambda w: pl.BlockSpec((rows, w), lambda b, j: (b * nc + j, 0))
    return pl.pallas_call(
        _ssd_kernel,
        out_shape=jax.ShapeDtypeStruct((n, D_INNER), BF),
        grid=(bsz, nc),
        in_specs=[
            blk(XBC_DIM), blk(D_INNER), blk(LANES),
            pl.BlockSpec((SSM_HEADS, rows), lambda b, j: (0, b * nc + j)),
            _full((CONV_WIDTH, XBC_DIM)), _full((1, XBC_DIM)),
            _full((1, LANES)), _full((SSM_HEADS, CHUNK)), _full((1, LANES)), _full((SSM_HEADS, CHUNK)),
            _full((1, D_INNER)), _full((1, D_INNER)),
        ],
        out_specs=blk(D_INNER),
        scratch_shapes=[pltpu.VMEM((SSM_HEADS // 2, D_STATE, LANES), F32),
                        pltpu.VMEM((SUBLANES, XBC_DIM), F32)],
        compiler_params=_params(("parallel", "arbitrary")),
        name="ssd",
    )(xbc, z, dt_raw, dt_row, conv_w, conv_b.reshape(1, -1), dtb_col, dtb_row, al_col, al_row,
      dskip, norm_w.reshape(1, -1))


def _outproj_kernel(x_ref, attn_ref, ssm_ref, g_ref, wa_ref, ws_ref, wo_ref, h_ref):
    a = _dot(attn_ref[...], wa_ref[...])
    s = _dot(ssm_ref[...], ws_ref[...])
    merged = g_ref[:, :D_MODEL].astype(F32) * a + g_ref[:, D_MODEL:].astype(F32) * s
    h_ref[...] = x_ref[...] + _dot(merged.astype(BF), wo_ref[...])


def _outproj(x2, attn, ssm, gates, w_attn_o, w_ssm_o, w_out):
    n = x2.shape[0]
    tm = OUT_TM
    row = lambda w: pl.BlockSpec((tm, w), lambda i: (i, 0))
    return pl.pallas_call(
        _outproj_kernel,
        out_shape=jax.ShapeDtypeStruct((n, D_MODEL), F32),
        grid=(n // tm,),
        in_specs=[row(D_MODEL), row(Q_DIM), row(D_INNER), row(2 * D_MODEL),
                  _full((Q_DIM, D_MODEL)), _full((D_INNER, D_MODEL)), _full((D_MODEL, D_MODEL))],
        out_specs=row(D_MODEL),
        compiler_params=_params(("parallel",)),
        name="outproj",
    )(x2, attn, ssm, gates, w_attn_o.astype(BF), w_ssm_o.astype(BF), w_out.astype(BF))


def _memkv_kernel(mem_ref, nw_ref, w_ref, kv_ref):
    mn = _rms(mem_ref[...], nw_ref[...]).astype(BF)
    kv_ref[...] = _dot(mn, w_ref[...]).astype(BF)


def _memkv(mem2, norm_w, w_ckv, bsz):
    return pl.pallas_call(
        _memkv_kernel,
        out_shape=jax.ShapeDtypeStruct((bsz * MEM_TOKENS, 2 * D_MODEL), BF),
        grid=(bsz,),
        in_specs=[pl.BlockSpec((MEM_TOKENS, D_MODEL), lambda b: (b, 0)), _full((1, D_MODEL)),
                  _full((D_MODEL, 2 * D_MODEL))],
        out_specs=pl.BlockSpec((MEM_TOKENS, 2 * D_MODEL), lambda b: (b, 0)),
        compiler_params=_params(("parallel",)),
        name="memkv",
    )(mem2, norm_w.reshape(1, -1), w_ckv.astype(BF))


def _cross_kernel(h_ref, kv_ref, nw_ref, wq_ref, wo_ref, o_ref):
    h = h_ref[...]
    hn = _rms(h, nw_ref[...]).astype(BF)
    q = (_dot(hn, wq_ref[...]) * (CROSS_HEAD_DIM ** -0.5)).astype(BF)
    outs = []
    for hd in range(CROSS_HEADS):
        lo, hi = CROSS_HEAD_DIM * hd, CROSS_HEAD_DIM * (hd + 1)
        s = _dot_nt(q[:, lo:hi], kv_ref[:, lo:hi])
        p = jnp.exp(s - jnp.max(s, axis=-1, keepdims=True))
        den = jnp.sum(p, axis=-1, keepdims=True)
        outs.append((_dot(p.astype(BF), kv_ref[:, D_MODEL + lo:D_MODEL + hi]) / den).astype(BF))
    o = jnp.concatenate(outs, axis=1)
    o_ref[...] = h + _dot(o, wo_ref[...])


def _cross(h, kv, norm_w, w_cq, w_co, seq):
    n = h.shape[0]
    tm = CROSS_TM
    per_b = seq // tm
    row = pl.BlockSpec((tm, D_MODEL), lambda i: (i, 0))
    return pl.pallas_call(
        _cross_kernel,
        out_shape=jax.ShapeDtypeStruct((n, D_MODEL), F32),
        grid=(n // tm,),
        in_specs=[row, pl.BlockSpec((MEM_TOKENS, 2 * D_MODEL), lambda i: (i // per_b, 0)),
                  _full((1, D_MODEL)), _full((D_MODEL, D_MODEL)), _full((D_MODEL, D_MODEL))],
        out_specs=row,
        compiler_params=_params(("parallel",)),
        name="cross",
    )(h, kv, norm_w.reshape(1, -1), w_cq.astype(BF), w_co.astype(BF))


def _router_kernel(h_ref, nw_ref, wr_ref, br_ref, t_ref, meta_ref, cnt_ref, carry_ref):
    i = pl.program_id(0)
    tm = ROUTER_TM

    @pl.when(i == 0)
    def _():
        carry_ref[...] = jnp.zeros_like(carry_ref)

    t = _rms(h_ref[...], nw_ref[...])
    for c in range(ROW_TILES):
        t_ref[pl.ds(c, tm, stride=ROW_TILES), :] = t[:, LANES * c:LANES * (c + 1)]
    lane = lax.broadcasted_iota(I32, (tm, LANES), 1)
    t_hi = t.astype(BF)
    t_lo = (t - t_hi.astype(F32)).astype(BF)
    w_hi = wr_ref[...].astype(BF)
    w_lo = (wr_ref[...] - w_hi.astype(F32)).astype(BF)
    raw = _dot(t_hi, w_hi) + (_dot(t_lo, w_hi) + _dot(t_hi, w_lo))
    logits = jnp.where(lane < N_EXPERTS, raw + br_ref[...], NEG_BIG)
    work = logits
    vals, ids, hots = [], [], []
    for _k in range(TOP_K):
        m = jnp.max(work, axis=-1, keepdims=True)
        idx = jnp.min(jnp.where(work == m, lane, LANES), axis=-1, keepdims=True)
        hot = lane == idx
        vals.append(m)
        ids.append(idx)
        hots.append(hot)
        work = jnp.where(hot, NEG_BIG * 2.0, work)
    es = [jnp.exp(v - vals[0]) for v in vals]
    den = es[0] + es[1] + es[2] + es[3]
    member = jnp.zeros((tm, LANES), F32)
    for hot in hots:
        member = member + jnp.where(hot, 1.0, 0.0)
    ii = lax.broadcasted_iota(I32, (tm, tm), 0)
    jj = lax.broadcasted_iota(I32, (tm, tm), 1)
    strict = jnp.where(ii > jj, 1.0, 0.0).astype(BF)
    carry = carry_ref[0:1, :]
    before = _dot(strict, member.astype(BF)) + carry
    meta = jnp.zeros((tm, LANES), F32)
    for k in range(TOP_K):
        rank = jnp.sum(jnp.where(hots[k], before, 0.0), axis=-1, keepdims=True)
        meta = jnp.where(lane == k, ids[k].astype(F32), meta)
        meta = jnp.where(lane == TOP_K + k, rank, meta)
        meta = jnp.where(lane == 2 * TOP_K + k, es[k] / den, meta)
    meta_ref[...] = meta
    new_carry = carry + jnp.sum(member, axis=0, keepdims=True)
    carry_ref[...] = jnp.broadcast_to(new_carry, carry_ref.shape)
    cnt_ref[...] = jnp.broadcast_to(new_carry, cnt_ref.shape)


def _router(h, norm_w, w_router, b_router):
    n = h.shape[0]
    tm = ROUTER_TM
    wr = jnp.pad(w_router, ((0, 0), (0, LANES - N_EXPERTS)))
    br = jnp.pad(b_router, (0, LANES - N_EXPERTS)).reshape(1, LANES)
    return pl.pallas_call(
        _router_kernel,
        out_shape=(jax.ShapeDtypeStruct((n * ROW_TILES, LANES), F32),
                   jax.ShapeDtypeStruct((n, LANES), F32),
                   jax.ShapeDtypeStruct((SUBLANES, LANES), F32)),
        grid=(n // tm,),
        in_specs=[pl.BlockSpec((tm, D_MODEL), lambda i: (i, 0)), _full((1, D_MODEL)),
                  _full((D_MODEL, LANES)), _full((1, LANES))],
        out_specs=[pl.BlockSpec((tm * ROW_TILES, LANES), lambda i: (i, 0)),
                   pl.BlockSpec((tm, LANES), lambda i: (i, 0)),
                   pl.BlockSpec((SUBLANES, LANES), lambda i: (0, 0))],
        scratch_shapes=[pltpu.VMEM((SUBLANES, LANES), F32)],
        compiler_params=_params(("arbitrary",)),
        name="router",
    )(h, norm_w.reshape(1, -1), wr, br)


def _row(ref, r):
    return ref.at[pl.ds(pl.multiple_of(r * ROW_TILES, ROW_TILES), ROW_TILES)]


def _dispatch_kernel(dest_ref, t_ref, xs_hbm, sem):
    tt = DISPATCH_TT

    def copy(j, k):
        return pltpu.make_async_copy(_row(t_ref, j), _row(xs_hbm, dest_ref[j * TOP_K + k]), sem)

    def issue(j, carry):
        for k in range(TOP_K):
            copy(j, k).start(priority=k % DMA_PRIORITIES)
        return carry

    def drain(j, carry):
        for k in range(TOP_K):
            copy(j, k).wait()
        return carry

    lax.fori_loop(0, tt, issue, 0)
    lax.fori_loop(0, tt, drain, 0)


def _dispatch(t_rows, dest):
    n = t_rows.shape[0] // ROW_TILES
    tt = DISPATCH_TT
    return pl.pallas_call(
        _dispatch_kernel,
        out_shape=jax.ShapeDtypeStruct((n * TOP_K * ROW_TILES, LANES), F32),
        grid=(n // tt,),
        in_specs=[pl.BlockSpec((tt * TOP_K,), lambda i: (i,), memory_space=pltpu.SMEM),
                  pl.BlockSpec((tt * ROW_TILES, LANES), lambda i: (i, 0))],
        out_specs=pl.BlockSpec(memory_space=pl.ANY),
        scratch_shapes=[pltpu.SemaphoreType.DMA],
        compiler_params=_params(("arbitrary",)),
        name="dispatch",
    )(dest, t_rows)


def _expert_kernel(tile_ref, exp_ref, lo_ref, hi_ref, newexp_ref, slot_ref, next_ref,
                   xs_ref, wgu_hbm, bgu_ref, wdn_hbm, bdn_ref, ys_ref, wgu_bf, wdn_bf, wgu_f32, wdn_f32, sems):
    w = pl.program_id(0)
    tm = EXPERT_TM
    lo = lo_ref[w]
    hi = hi_ref[w]

    def weight_copies(e, slot):
        return (pltpu.make_async_copy(wgu_hbm.at[e], wgu_f32.at[slot], sems.at[0, slot]),
                pltpu.make_async_copy(wdn_hbm.at[e], wdn_f32.at[slot], sems.at[1, slot]))

    @pl.when(w == 0)
    def _():
        for cp in weight_copies(exp_ref[0], 0):
            cp.start()

    @pl.when(newexp_ref[w] == 1)
    def _():
        slot = slot_ref[w]
        for cp in weight_copies(exp_ref[w], slot):
            cp.wait()
        wgu_bf[...] = wgu_f32[slot].astype(BF)
        wdn_bf[...] = wdn_f32[slot].astype(BF)

        @pl.when(next_ref[w] >= 0)
        def _():
            for cp in weight_copies(next_ref[w], 1 - slot):
                cp.start()

    sb = EXPERT_SUB
    for part in range(tm // sb):
        start = tile_ref[w] * tm + part * sb

        @pl.when(jnp.minimum(hi, start + sb) > jnp.maximum(lo, start))
        def _(part=part, start=start):
            chunk = lambda c: pl.ds(part * sb * ROW_TILES + c, sb, stride=ROW_TILES)
            x = jnp.concatenate([xs_ref[chunk(c), :] for c in range(ROW_TILES)], axis=1).astype(BF)
            gu = _dot(x, wgu_bf[...]) + bgu_ref[0]
            g = jnp.minimum(gu[:, :D_EXPERT], SWIGLU_LIMIT)
            up = jnp.clip(gu[:, D_EXPERT:], -SWIGLU_LIMIT, SWIGLU_LIMIT)
            hg = 0.5 * g
            act = (up + 1.0) * (hg + hg * jnp.tanh(SWIGLU_ALPHA * hg))
            y = _dot(act.astype(BF), wdn_bf[...]) + bdn_ref[0]
            rows = start + lax.broadcasted_iota(I32, (sb, LANES), 0)
            mine = (rows >= lo) & (rows < hi)
            is_first = lo <= start

            @pl.when(is_first)
            def _():
                for c in range(ROW_TILES):
                    ys_ref[chunk(c), :] = jnp.where(mine, y[:, LANES * c:LANES * (c + 1)], 0.0)

            @pl.when(jnp.logical_not(is_first))
            def _():
                for c in range(ROW_TILES):
                    ys_ref[chunk(c), :] = jnp.where(mine, y[:, LANES * c:LANES * (c + 1)], ys_ref[chunk(c), :])


def _experts(xs, items, w_gu, b_gu, w_dn, b_dn):
    n_items = items[0].shape[0]
    tm = EXPERT_TM
    grid_spec = pltpu.PrefetchScalarGridSpec(
        num_scalar_prefetch=len(items),
        grid=(n_items,),
        in_specs=[
            pl.BlockSpec((tm * ROW_TILES, LANES), lambda w, t, e, *_: (t[w], 0)),
            pl.BlockSpec(memory_space=pl.ANY),
            pl.BlockSpec((1, 1, 2 * D_EXPERT), lambda w, t, e, *_: (e[w], 0, 0)),
            pl.BlockSpec(memory_space=pl.ANY),
            pl.BlockSpec((1, 1, D_MODEL), lambda w, t, e, *_: (e[w], 0, 0)),
        ],
        out_specs=pl.BlockSpec((tm * ROW_TILES, LANES), lambda w, t, e, *_: (t[w], 0)),
        scratch_shapes=[pltpu.VMEM((D_MODEL, 2 * D_EXPERT), BF), pltpu.VMEM((D_EXPERT, D_MODEL), BF),
                        pltpu.VMEM((2, D_MODEL, 2 * D_EXPERT), F32), pltpu.VMEM((2, D_EXPERT, D_MODEL), F32),
                        pltpu.SemaphoreType.DMA((2, 2))],
    )
    return pl.pallas_call(
        _expert_kernel,
        out_shape=jax.ShapeDtypeStruct(xs.shape, F32),
        grid_spec=grid_spec,
        compiler_params=_params(("arbitrary",)),
        name="experts",
    )(*items, xs, w_gu, b_gu.reshape(N_EXPERTS, 1, -1), w_dn, b_dn.reshape(N_EXPERTS, 1, -1))


def _expert_items(counts, n_rows):
    tm = EXPERT_TM
    n_tiles = n_rows // tm
    n_items = n_tiles + N_EXPERTS - 1
    end = jnp.cumsum(counts)
    off = end - counts
    first_tile = off // tm
    last_tile = jnp.maximum(end - 1, 0) // tm
    per = jnp.where(counts > 0, last_tile - first_tile + 1, 0)
    cum = jnp.cumsum(per)
    start = cum - per
    total = cum[-1]
    w = jnp.arange(n_items, dtype=I32)
    wc = jnp.minimum(w, total - 1)
    e_of = jnp.minimum(jnp.sum((cum[None, :] <= wc[:, None]).astype(I32), axis=1), N_EXPERTS - 1)
    sel = e_of[:, None] == jnp.arange(N_EXPERTS, dtype=I32)
    pick = lambda v: jnp.sum(jnp.where(sel, v[None, :], 0), axis=1)
    t_of = (pick(first_tile) + (wc - pick(start))).astype(I32)
    valid = w < total
    lo = jnp.where(valid, jnp.maximum(pick(off), t_of * tm), 0).astype(I32)
    hi = jnp.where(valid, jnp.minimum(pick(end), (t_of + 1) * tm), 0).astype(I32)
    prev_e = jnp.concatenate([jnp.full((1,), -1, I32), e_of[:-1]])
    newexp = (e_of != prev_e).astype(I32)
    slot = (jnp.cumsum(newexp) - 1) % 2
    later_change = (w[None, :] > w[:, None]) & (newexp[None, :] == 1)
    nxt = jnp.min(jnp.where(later_change, w[None, :], n_items), axis=1)
    next_e = jnp.sum(jnp.where(nxt[:, None] == w[None, :], e_of[None, :], 0), axis=1)
    next_e = jnp.where(nxt < n_items, next_e, -1).astype(I32)
    return t_of, e_of, lo, hi, newexp, slot.astype(I32), next_e


def _combine_kernel(dest_ref, meta_ref, h_ref, fw_ref, ys_hbm, o_ref, buf_ref, sems):
    tt = COMBINE_TT
    tp = tt // COMBINE_PHASES

    def copy(j, k, p):
        return pltpu.make_async_copy(_row(ys_hbm, dest_ref[j * TOP_K + k]), _row(buf_ref, k * tt + j), sems.at[p])

    for p in range(COMBINE_PHASES):
        def issue(j, carry, p=p):
            for k in range(TOP_K):
                copy(j, k, p).start(priority=k % DMA_PRIORITIES)
            return carry
        lax.fori_loop(p * tp, (p + 1) * tp, issue, 0)

    for p in range(COMBINE_PHASES):
        def drain(j, carry, p=p):
            for k in range(TOP_K):
                copy(j, k, p).wait()
            return carry
        lax.fori_loop(p * tp, (p + 1) * tp, drain, 0)
        rows = pl.ds(p * tp, tp)
        meta = meta_ref[rows, :]
        wts = [jnp.broadcast_to(meta[:, 2 * TOP_K + k:2 * TOP_K + k + 1], (tp, LANES)) for k in range(TOP_K)]
        cols = []
        for c in range(ROW_TILES):
            acc = h_ref[rows, LANES * c:LANES * (c + 1)]
            for k in range(TOP_K):
                acc = acc + wts[k] * buf_ref[pl.ds((k * tt + p * tp) * ROW_TILES + c, tp, stride=ROW_TILES), :]
            cols.append(acc)
        o_ref[rows, :] = _rms(jnp.concatenate(cols, axis=1), fw_ref[...])


def _combine(dest, meta, h, final_w, ys):
    n = h.shape[0]
    tt = COMBINE_TT
    return pl.pallas_call(
        _combine_kernel,
        out_shape=jax.ShapeDtypeStruct((n, D_MODEL), F32),
        grid=(n // tt,),
        in_specs=[pl.BlockSpec((tt * TOP_K,), lambda i: (i,), memory_space=pltpu.SMEM),
                  pl.BlockSpec((tt, LANES), lambda i: (i, 0)),
                  pl.BlockSpec((tt, D_MODEL), lambda i: (i, 0)),
                  _full((1, D_MODEL)),
                  pl.BlockSpec(memory_space=pl.ANY)],
        out_specs=pl.BlockSpec((tt, D_MODEL), lambda i: (i, 0)),
        scratch_shapes=[pltpu.VMEM((TOP_K * tt * ROW_TILES, LANES), F32),
                        pltpu.SemaphoreType.DMA((COMBINE_PHASES,))],
        compiler_params=_params(("arbitrary",)),
        name="combine",
    )(dest, meta, h, final_w.reshape(1, -1), ys)


def _moe(h, norm_w, w_router, b_router, w_gu, b_gu, w_dn, b_dn, final_w):
    n = h.shape[0]
    t_rows, meta, cnt = _router(h, norm_w, w_router, b_router)
    eid = meta[:, :TOP_K].astype(I32)
    rank = meta[:, TOP_K:2 * TOP_K].astype(I32)
    counts = cnt[0, :N_EXPERTS].astype(I32)
    off = jnp.cumsum(counts) - counts
    hot = eid[..., None] == jnp.arange(N_EXPERTS, dtype=I32)
    dest = (jnp.sum(jnp.where(hot, off, 0), axis=-1) + rank).reshape(-1)
    xs = _dispatch(t_rows, dest)
    ys = _experts(xs, _expert_items(counts, n * TOP_K), w_gu, b_gu, w_dn, b_dn)
    return _combine(dest, meta, h, final_w, ys)


def kernel(x, mem, norm_mix_w, w_in, w_gate, b_gate, attn_sinks, conv_w, conv_b, dt_bias, a_log, d_skip,
           ssm_norm_w, w_attn_o, w_ssm_o, w_out, norm_cross_w, norm_mem_w, w_cq, w_ckv, w_co,
           norm_ffn_w, w_router, b_router, w_gu, b_gu, w_dn, b_dn, final_norm_w):
    bsz, seq, _ = x.shape
    assert norm_mix_w.shape[0] == 1, "single-layer block"
    x2 = x.reshape(bsz * seq, D_MODEL)
    q, kv, z, xbc, dt_raw, gates = _proj(x2, norm_mix_w[0], w_in[0], w_gate[0], b_gate[0])
    attn = _swa(q, kv, attn_sinks[0], bsz, seq)
    ssm = _ssd(xbc, z, dt_raw, conv_w[0], conv_b[0], dt_bias[0], a_log[0], d_skip[0], ssm_norm_w[0], bsz, seq)
    h = _outproj(x2, attn, ssm, gates, w_attn_o[0], w_ssm_o[0], w_out[0])
    mkv = _memkv(mem.reshape(bsz * MEM_TOKENS, D_MODEL), norm_mem_w[0], w_ckv[0], bsz)
    h = _cross(h, mkv, norm_cross_w[0], w_cq[0], w_co[0], seq)
    out = _moe(h, norm_ffn_w[0], w_router[0], b_router[0], w_gu[0], b_gu[0], w_dn[0], b_dn[0], final_norm_w)
    return out.reshape(bsz, seq, D_MODEL)
```
